```python
import math
import jax, jax.numpy as jnp
from jax import lax
import numpy as np

D_MODEL = 4096
BATCH = 1
SEQ = 16384
DEPTH = 4
DEC_BATCH = 1
DEC_SEQ = 8192
PAST_LEN = 128

GLA_HEADS = 4
GLA_DK = 128
GLA_DV = 256
GLA_RANK = 16
GLA_TEMP = 16.0
GLA_CHUNK = 64
ATT_GROUPS = ((128, 1), (512, 4), (2048, 16))
ATT_HEADS_PER_GROUP = 8
ATT_HEAD_DIM = 128
CONV_CH = 1024
CONV_WIDTH = 31
N_BRANCH = 3
BRANCH_WIDTH = 1024
N_EXPERTS = 16
EXPERT_FF = 2048
EC_FACTOR = 2

LN_EPS = 1e-5
NEG_INF = -1e30

GLA_QK = GLA_HEADS * GLA_DK
GLA_V = GLA_HEADS * GLA_DV
N_ATT_GROUPS = len(ATT_GROUPS)
ATT_QKV = N_ATT_GROUPS * ATT_HEADS_PER_GROUP * ATT_HEAD_DIM
ATT_OUT = ATT_HEADS_PER_GROUP * ATT_HEAD_DIM
IN_WIDTHS = (GLA_QK, GLA_QK, GLA_V, GLA_V, 2 * GLA_RANK, ATT_QKV, ATT_QKV, ATT_QKV, 2 * CONV_CH)
IN_COLS = sum(IN_WIDTHS)
IN_OFFSETS = tuple(int(v) for v in np.cumsum(IN_WIDTHS)[:-1])

kernel_name = "hybrid_gla_dilated_conformer_ec_encoder"


def layer_norm(x, g, b):
    xf = x.astype(jnp.float32)
    mu = jnp.mean(xf, axis=-1, keepdims=True)
    var = jnp.mean(jnp.square(xf - mu), axis=-1, keepdims=True)
    return ((xf - mu) * lax.rsqrt(var + LN_EPS) * g.astype(jnp.float32) + b.astype(jnp.float32)).astype(x.dtype)


def head_rms_norm(x, g):
    return x * lax.rsqrt(jnp.mean(jnp.square(x), axis=-1, keepdims=True) + LN_EPS) * g.astype(jnp.float32)


def alibi_slopes(n):
    return jnp.asarray(2.0 ** (-8.0 * np.arange(1, n + 1) / n), dtype=jnp.float32)


def gla_chunked(q, k, v, log_a):
    bsz, seqlen, nh, dk = q.shape
    dv = v.shape[-1]
    n_chunk = seqlen // GLA_CHUNK

    def to_chunks(a):
        return a.reshape(bsz, n_chunk, GLA_CHUNK, nh, a.shape[-1]).transpose(0, 3, 1, 2, 4)

    qc, kc, vc, lc = to_chunks(q), to_chunks(k), to_chunks(v), to_chunks(log_a)
    bcum = jnp.cumsum(lc, axis=3)
    b_last = bcum[:, :, :, -1:, :]
    q_e = qc * jnp.exp(bcum)
    k_e = kc * jnp.exp(-bcum)
    tril = jnp.tril(jnp.ones((GLA_CHUNK, GLA_CHUNK), dtype=bool))
    attn = jnp.where(tril, jnp.einsum('bhnid,bhnjd->bhnij', q_e, k_e), 0.0)
    o_intra = jnp.einsum('bhnij,bhnjv->bhniv', attn, vc)
    k_s = kc * jnp.exp(b_last - bcum)
    d_state = jnp.einsum('bhnjd,bhnjv->bhndv', k_s, vc)
    decay = jnp.exp(b_last[:, :, :, 0, :])

    def step(state, inp):
        dec, ds = inp
        return dec[..., None] * state + ds, state

    s0 = jnp.zeros((bsz, nh, dk, dv), jnp.float32)
    _, s_prev = lax.scan(step, s0, (jnp.moveaxis(decay, 2, 0), jnp.moveaxis(d_state, 2, 0)))
    s_prev = jnp.moveaxis(s_prev, 0, 2)
    o = o_intra + jnp.einsum('bhnid,bhndv->bhniv', q_e, s_prev)
    return o.transpose(0, 2, 3, 1, 4).reshape(bsz, seqlen, nh, dv)


def dilated_band_attention(q, k, v, slopes, dilation, window):
    bsz, seqlen, nh, dh = q.shape
    rad = window // (2 * dilation)
    n_res = seqlen // dilation
    n_blk = -(-n_res // rad)
    u_pad = n_blk * rad

    def to_res(a):
        return a.astype(jnp.float32).reshape(bsz, n_res, dilation, nh, dh).transpose(0, 2, 1, 3, 4)

    qr = jnp.pad(to_res(q), ((0, 0), (0, 0), (0, u_pad - n_res), (0, 0), (0, 0)))
    qr = qr.reshape(bsz, dilation, n_blk, rad, nh, dh)

    def key_blocks(a):
        ap = jnp.pad(to_res(a), ((0, 0), (0, 0), (rad, u_pad - n_res + rad), (0, 0), (0, 0)))
        ap = ap.reshape(bsz, dilation, n_blk + 2, rad, nh, dh)
        return jnp.concatenate([ap[:, :, :-2], ap[:, :, 1:-1], ap[:, :, 2:]], axis=3)

    kb, vb = key_blocks(k), key_blocks(v)
    s = jnp.einsum('brnqhc,brnkhc->brnhqk', qr, kb) * (dh ** -0.5)
    u_q = jnp.arange(u_pad).reshape(n_blk, rad)[:, :, None]
    u_k = (jnp.arange(n_blk)[:, None] * rad - rad + jnp.arange(3 * rad)[None, :])[:, None, :]
    dist = jnp.abs(u_q - u_k)
    valid = (dist <= rad) & (u_k >= 0) & (u_k < n_res)
    bias = -(slopes[None, :, None, None] * (dilation * dist).astype(jnp.float32)[:, None])
    s = jnp.where(valid[:, None], s + bias, NEG_INF)
    lse = jax.nn.logsumexp(s, axis=-1)
    p = jnp.exp(s - lse[..., None])
    o = jnp.einsum('brnhqk,brnkhc->brnqhc', p, vb)
    o = o.reshape(bsz, dilation, u_pad, nh, dh)[:, :, :n_res].transpose(0, 2, 1, 3, 4).reshape(bsz, seqlen, nh, dh)
    lse = lse.transpose(0, 1, 2, 4, 3).reshape(bsz, dilation, u_pad, nh)[:, :, :n_res]
    lse = lse.transpose(0, 2, 1, 3).reshape(bsz, seqlen, nh)
    return o, lse


def mixer_block(x, w_in, w_gla_gate, b_gla_gate, gla_norm_g, conv_w, conv_b, conv_ln_g, conv_ln_b,
                w_branch, w_gate, b_gate, w_out):
    bsz, seqlen, _ = x.shape
    f32 = jnp.float32
    h = jnp.einsum('btd,dc->btc', x, w_in)
    qa, ka, va, ga, ra, qb, kb, vb, uc = jnp.split(h, IN_OFFSETS, axis=-1)

    qa = qa.reshape(bsz, seqlen, GLA_HEADS, GLA_DK).astype(f32) * (GLA_DK ** -0.5)
    ka = ka.reshape(bsz, seqlen, GLA_HEADS, GLA_DK).astype(f32)
    va = va.reshape(bsz, seqlen, GLA_HEADS, GLA_DV).astype(f32)
    ra = ra.reshape(bsz, seqlen, 2, GLA_RANK)
    z = jnp.einsum('btjr,jrk->btjk', ra, w_gla_gate) + b_gla_gate
    log_a = (jax.nn.log_sigmoid(z.astype(f32)) / GLA_TEMP).reshape(bsz, seqlen, 2, GLA_HEADS, GLA_DK)
    o_fwd = gla_chunked(qa, ka, va, log_a[:, :, 0])
    flip = lambda a: jnp.flip(a, axis=1)
    o_bwd = flip(gla_chunked(flip(qa), flip(ka), flip(va), flip(log_a[:, :, 1])))
    o_a = head_rms_norm(o_fwd + o_bwd, gla_norm_g) * jax.nn.silu(ga.reshape(bsz, seqlen, GLA_HEADS, GLA_DV).astype(f32))
    br_a = jnp.einsum('btc,cd->btd', o_a.reshape(bsz, seqlen, GLA_V).astype(x.dtype), w_branch[0])

    grp_shape = (bsz, seqlen, N_ATT_GROUPS, ATT_HEADS_PER_GROUP, ATT_HEAD_DIM)
    qb, kb, vb = qb.reshape(grp_shape), kb.reshape(grp_shape), vb.reshape(grp_shape)
    slopes = alibi_slopes(N_ATT_GROUPS * ATT_HEADS_PER_GROUP).reshape(N_ATT_GROUPS, ATT_HEADS_PER_GROUP)
    outs, lses = [], []
    for g, (win, dil) in enumerate(ATT_GROUPS):
        o_g, l_g = dilated_band_attention(qb[:, :, g], kb[:, :, g], vb[:, :, g], slopes[g], dil, win)
        outs.append(o_g)
        lses.append(l_g)
    o_stack = jnp.stack(outs, axis=2)
    wts = jax.nn.softmax(jnp.stack(lses, axis=2), axis=2)
    o_b = jnp.sum(wts[..., None] * o_stack, axis=2).reshape(bsz, seqlen, ATT_OUT).astype(x.dtype)
    br_b = jnp.einsum('btc,cd->btd', o_b, w_branch[1])

    u_val, u_gate = jnp.split(uc, 2, axis=-1)
    u = u_val * jax.nn.sigmoid(u_gate)
    u = lax.conv_general_dilated(u, conv_w.reshape(CONV_WIDTH, 1, CONV_CH).astype(u.dtype),
                                 window_strides=(1,), padding=[(CONV_WIDTH // 2, CONV_WIDTH // 2)],
                                 dimension_numbers=('NWC', 'WIO', 'NWC'), feature_group_count=CONV_CH) + conv_b
    u = jax.nn.silu(layer_norm(u, conv_ln_g, conv_ln_b))
    br_c = jnp.einsum('btc,cd->btd', u, w_branch[2])

    gates = jax.nn.sigmoid(jnp.einsum('btd,dc->btc', x, w_gate) + b_gate).reshape(bsz, seqlen, N_BRANCH, D_MODEL)
    merged = gates[:, :, 0] * br_a + gates[:, :, 1] * br_b + gates[:, :, 2] * br_c
    return jnp.einsum('btd,de->bte', merged.astype(x.dtype), w_out)


def expert_choice_moe(x2d, w_router, w_e1, w_e3, w_e2):
    n_tok = x2d.shape[0]
    cap = max(1, EC_FACTOR * n_tok // N_EXPERTS)
    probs = jax.nn.softmax(jnp.einsum('td,de->te', x2d, w_router).astype(jnp.float32), axis=-1)
    gate, idx = lax.top_k(probs.T, cap)
    xe = jnp.take(x2d, idx, axis=0)
    hid = jax.nn.silu(jnp.einsum('ecd,edf->ecf', xe, w_e1)) * jnp.einsum('ecd,edf->ecf', xe, w_e3)
    ye = jnp.einsum('ecf,efd->ecd', hid, w_e2) * gate[..., None].astype(x2d.dtype)
    return jnp.zeros_like(x2d).at[idx.reshape(-1)].add(ye.reshape(-1, x2d.shape[-1]).astype(x2d.dtype))


def encoder_trunk(x, w_in, w_gla_gate, b_gla_gate, gla_norm_g, conv_w, conv_b, conv_ln_g, conv_ln_b,
                  w_branch, w_gate, b_gate, w_out, ln1_g, ln1_b, w_router, w_e1, w_e3, w_e2, ln2_g, ln2_b):
    alpha = (2.0 * DEPTH) ** 0.25
    for l in range(DEPTH):
        y = mixer_block(x, w_in[l], w_gla_gate[l], b_gla_gate[l], gla_norm_g[l], conv_w[l], conv_b[l],
                        conv_ln_g[l], conv_ln_b[l], w_branch[l], w_gate[l], b_gate[l], w_out[l])
        x = layer_norm(alpha * x + y, ln1_g[l], ln1_b[l])
        m = expert_choice_moe(x.reshape(-1, D_MODEL), w_router[l], w_e1[l], w_e3[l], w_e2[l]).reshape(x.shape)
        x = layer_norm(alpha * x + m, ln2_g[l], ln2_b[l])
    return x


def setup_inputs(seed: int = 0) -> dict:
    key = jax.random.key(seed)
    ks = jax.random.split(key, 24)
    beta = (8.0 * DEPTH) ** -0.25
    f32 = jnp.float32
    nrm = lambda k, shape, scale: jax.random.normal(k, shape, f32) * scale
    return {
        'x_prompt': nrm(ks[0], (BATCH, SEQ, D_MODEL), 1.0),
        'x_sample': nrm(ks[1], (DEC_BATCH, DEC_SEQ, D_MODEL), 1.0),
        'w_in': nrm(ks[2], (DEPTH, D_MODEL, IN_COLS), D_MODEL ** -0.5),
        'w_gla_gate': nrm(ks[3], (DEPTH, 2, GLA_RANK, GLA_QK), GLA_RANK ** -0.5),
        'b_gla_gate': nrm(ks[4], (DEPTH, 2, GLA_QK), 0.1),
        'gla_norm_g': 1.0 + nrm(ks[5], (DEPTH, GLA_DV), 0.02),
        'conv_w': nrm(ks[6], (DEPTH, CONV_WIDTH, CONV_CH), CONV_WIDTH ** -0.5),
        'conv_b': nrm(ks[7], (DEPTH, CONV_CH), 0.02),
        'conv_ln_g': 1.0 + nrm(ks[8], (DEPTH, CONV_CH), 0.02),
        'conv_ln_b': nrm(ks[9], (DEPTH, CONV_CH), 0.02),
        'w_branch': nrm(ks[10], (DEPTH, N_BRANCH, BRANCH_WIDTH, D_MODEL), beta * BRANCH_WIDTH ** -0.5),
        'w_gate': nrm(ks[11], (DEPTH, D_MODEL, N_BRANCH * D_MODEL), D_MODEL ** -0.5),
        'b_gate': nrm(ks[12], (DEPTH, N_BRANCH * D_MODEL), 0.02),
        'w_out': nrm(ks[13], (DEPTH, D_MODEL, D_MODEL), beta * D_MODEL ** -0.5),
        'ln1_g': 1.0 + nrm(ks[14], (DEPTH, D_MODEL), 0.02),
        'ln1_b': nrm(ks[15], (DEPTH, D_MODEL), 0.02),
        'w_router': nrm(ks[16], (DEPTH, D_MODEL, N_EXPERTS), D_MODEL ** -0.5),
        'w_e1': nrm(ks[17], (DEPTH, N_EXPERTS, D_MODEL, EXPERT_FF), D_MODEL ** -0.5),
        'w_e3': nrm(ks[18], (DEPTH, N_EXPERTS, D_MODEL, EXPERT_FF), D_MODEL ** -0.5),
        'w_e2': nrm(ks[19], (DEPTH, N_EXPERTS, EXPERT_FF, D_MODEL), beta * EXPERT_FF ** -0.5),
        'ln2_g': 1.0 + nrm(ks[20], (DEPTH, D_MODEL), 0.02),
        'ln2_b': nrm(ks[21], (DEPTH, D_MODEL), 0.02),
    }


def reference(x_prompt, x_sample, w_in, w_gla_gate, b_gla_gate, gla_norm_g, conv_w, conv_b, conv_ln_g, conv_ln_b,
              w_branch, w_gate, b_gate, w_out, ln1_g, ln1_b, w_router, w_e1, w_e3, w_e2, ln2_g, ln2_b):
    y_prompt = encoder_trunk(x_prompt, w_in, w_gla_gate, b_gla_gate, gla_norm_g, conv_w, conv_b, conv_ln_g, conv_ln_b,
                             w_branch, w_gate, b_gate, w_out, ln1_g, ln1_b, w_router, w_e1, w_e3, w_e2, ln2_g, ln2_b)
    y_sample = encoder_trunk(x_sample, w_in, w_gla_gate, b_gla_gate, gla_norm_g, conv_w, conv_b, conv_ln_g, conv_ln_b,
                             w_branch, w_gate, b_gate, w_out, ln1_g, ln1_b, w_router, w_e1, w_e3, w_e2, ln2_g, ln2_b)
    return (y_prompt, y_sample)
```

```python
import functools
import math

import jax
import jax.numpy as jnp
import numpy as np
from jax import lax
from jax.experimental import pallas as pl
from jax.experimental.pallas import tpu as pltpu

D_MODEL = 4096
DEPTH = 4
GLA_HEADS = 4
GLA_DK = 128
GLA_DV = 256
GLA_RANK = 16
GLA_TEMP = 16.0
GLA_CHUNK = 64
ATT_GROUPS = ((128, 1), (512, 4), (2048, 16))
ATT_HEADS_PER_GROUP = 8
ATT_HEAD_DIM = 128
CONV_CH = 1024
CONV_WIDTH = 31
N_BRANCH = 3
BRANCH_WIDTH = 1024
N_EXPERTS = 16
EXPERT_FF = 2048
EC_FACTOR = 2
LN_EPS = 1e-5
NEG_INF = -1e30

GLA_QK = GLA_HEADS * GLA_DK
GLA_V = GLA_HEADS * GLA_DV
N_ATT_GROUPS = len(ATT_GROUPS)
ATT_QKV = N_ATT_GROUPS * ATT_HEADS_PER_GROUP * ATT_HEAD_DIM
ATT_OUT = ATT_HEADS_PER_GROUP * ATT_HEAD_DIM
IN_WIDTHS = (GLA_QK, GLA_QK, GLA_V, GLA_V, 2 * GLA_RANK, ATT_QKV, ATT_QKV, ATT_QKV, 2 * CONV_CH)
IN_OFFSETS = tuple(int(v) for v in np.cumsum(IN_WIDTHS)[:-1])

V7X_VMEM_LIMIT_BYTES = 56 * 1024 * 1024

BF16 = jnp.bfloat16
F32 = jnp.float32


def _cparams(sem):
    return pltpu.CompilerParams(dimension_semantics=sem, vmem_limit_bytes=V7X_VMEM_LIMIT_BYTES)


def _mm_body(x_ref, w_ref, o_ref):
    o_ref[...] = jnp.dot(x_ref[...], w_ref[...], preferred_element_type=F32).astype(o_ref.dtype)


def matmul(x, w, out_dtype, tm=1024, tn=1024):
    m, k = x.shape
    n = w.shape[1]
    tm, tn = min(tm, m), min(tn, n)
    assert m % tm == 0 and n % tn == 0
    return pl.pallas_call(
        _mm_body,
        grid=(m // tm, n // tn),
        in_specs=[pl.BlockSpec((tm, k), lambda i, j: (i, 0)),
                  pl.BlockSpec((k, tn), lambda i, j: (0, j))],
        out_specs=pl.BlockSpec((tm, tn), lambda i, j: (i, j)),
        out_shape=jax.ShapeDtypeStruct((m, n), out_dtype),
        compiler_params=_cparams(("parallel", "parallel")),
    )(x, w)


def _merge_body(x_ref, oa_ref, ob_ref, oc_ref, wg_ref, bg_ref, wb_ref, o_ref):
    x = x_ref[...]
    acc = None
    for i, o_i in enumerate((oa_ref, ob_ref, oc_ref)):
        g = jnp.dot(x, wg_ref[i], preferred_element_type=F32) + bg_ref[i]
        br = jnp.dot(o_i[...], wb_ref[i], preferred_element_type=F32)
        term = jax.nn.sigmoid(g) * br
        acc = term if acc is None else acc + term
    o_ref[...] = acc.astype(o_ref.dtype)


def gated_merge(x, o_a, o_b, o_c, w_gate3, b_gate3, w_branch, tm=512, tn=512):
    t, d = x.shape
    bw = o_a.shape[1]
    o_spec = pl.BlockSpec((tm, bw), lambda i, j: (i, 0))
    return pl.pallas_call(
        _merge_body,
        grid=(t // tm, d // tn),
        in_specs=[pl.BlockSpec((tm, d), lambda i, j: (i, 0)), o_spec, o_spec, o_spec,
                  pl.BlockSpec((N_BRANCH, d, tn), lambda i, j: (0, 0, j)),
                  pl.BlockSpec((N_BRANCH, 1, tn), lambda i, j: (0, 0, j)),
                  pl.BlockSpec((N_BRANCH, bw, tn), lambda i, j: (0, 0, j))],
        out_specs=pl.BlockSpec((tm, tn), lambda i, j: (i, j)),
        out_shape=jax.ShapeDtypeStruct((t, d), BF16),
        compiler_params=_cparams(("parallel", "parallel")),
    )(x, o_a, o_b, o_c, w_gate3, b_gate3, w_branch)


def _ffn_body(xe_ref, g_ref, w1_ref, w3_ref, w2_ref, o_ref):
    f = pl.program_id(2)
    xe = xe_ref[0]
    h1 = jnp.dot(xe, w1_ref[0], preferred_element_type=F32)
    h3 = jnp.dot(xe, w3_ref[0], preferred_element_type=F32)
    hid = (jax.nn.silu(h1) * h3).astype(BF16)
    part = jnp.dot(hid, w2_ref[0], preferred_element_type=F32)

    @pl.when(f == 0)
    def _():
        o_ref[0] = part

    @pl.when(f > 0)
    def _():
        o_ref[0] += part

    @pl.when(f == pl.num_programs(2) - 1)
    def _():
        o_ref[0] = o_ref[0] * g_ref[0]


def expert_ffn(xe, gate, w1, w3, w2, tm=512, tf=256):
    e, c, d = xe.shape
    ff = w1.shape[2]
    tm = min(tm, c)
    return pl.pallas_call(
        _ffn_body,
        grid=(e, c // tm, ff // tf),
        in_specs=[pl.BlockSpec((1, tm, d), lambda a, i, f: (a, i, 0)),
                  pl.BlockSpec((1, tm, 1), lambda a, i, f: (a, i, 0)),
                  pl.BlockSpec((1, d, tf), lambda a, i, f: (a, 0, f)),
                  pl.BlockSpec((1, d, tf), lambda a, i, f: (a, 0, f)),
                  pl.BlockSpec((1, tf, d), lambda a, i, f: (a, f, 0))],
        out_specs=pl.BlockSpec((1, tm, d), lambda a, i, f: (a, i, 0)),
        out_shape=jax.ShapeDtypeStruct((e, c, d), F32),
        compiler_params=_cparams(("parallel", "parallel", "arbitrary")),
    )(xe, gate, w1, w3, w2)


def layer_norm(x, g, b):
    xf = x.astype(F32)
    mu = jnp.mean(xf, axis=-1, keepdims=True)
    var = jnp.mean(jnp.square(xf - mu), axis=-1, keepdims=True)
    return (xf - mu) * lax.rsqrt(var + LN_EPS) * g + b


def head_rms_norm(x, g):
    return x * lax.rsqrt(jnp.mean(jnp.square(x), axis=-1, keepdims=True) + LN_EPS) * g


def alibi_slopes(n):
    return jnp.asarray(2.0 ** (-8.0 * np.arange(1, n + 1) / n), dtype=F32)


def gla_chunked(q, k, v, log_a):
    bsz, seqlen, nh, dk = q.shape
    dv = v.shape[-1]
    n_chunk = seqlen // GLA_CHUNK

    def to_chunks(a):
        return a.reshape(bsz, n_chunk, GLA_CHUNK, nh, a.shape[-1]).transpose(0, 3, 1, 2, 4)

    qc, kc, vc, lc = to_chunks(q), to_chunks(k), to_chunks(v), to_chunks(log_a)
    bcum = jnp.cumsum(lc, axis=3)
    b_last = bcum[:, :, :, -1:, :]
    q_e = qc * jnp.exp(bcum)
    k_e = kc * jnp.exp(-bcum)
    tril = jnp.tril(jnp.ones((GLA_CHUNK, GLA_CHUNK), dtype=bool))
    attn = jnp.where(tril, jnp.einsum('bhnid,bhnjd->bhnij', q_e, k_e), 0.0)
    o_intra = jnp.einsum('bhnij,bhnjv->bhniv', attn, vc)
    k_s = kc * jnp.exp(b_last - bcum)
    d_state = jnp.einsum('bhnjd,bhnjv->bhndv', k_s, vc)
    decay = jnp.exp(b_last[:, :, :, 0, :])

    def step(state, inp):
        dec, ds = inp
        return dec[..., None] * state + ds, state

    s0 = jnp.zeros((bsz, nh, dk, dv), F32)
    _, s_prev = lax.scan(step, s0, (jnp.moveaxis(decay, 2, 0), jnp.moveaxis(d_state, 2, 0)))
    s_prev = jnp.moveaxis(s_prev, 0, 2)
    o = o_intra + jnp.einsum('bhnid,bhndv->bhniv', q_e, s_prev)
    return o.transpose(0, 2, 3, 1, 4).reshape(bsz, seqlen, nh, dv)


def dilated_band_attention(q, k, v, slopes, dilation, window):
    bsz, seqlen, nh, dh = q.shape
    rad = window // (2 * dilation)
    n_res = seqlen // dilation
    n_blk = -(-n_res // rad)
    u_pad = n_blk * rad

    def to_res(a):
        return a.astype(F32).reshape(bsz, n_res, dilation, nh, dh).transpose(0, 2, 1, 3, 4)

    qr = jnp.pad(to_res(q), ((0, 0), (0, 0), (0, u_pad - n_res), (0, 0), (0, 0)))
    qr = qr.reshape(bsz, dilation, n_blk, rad, nh, dh)

    def key_blocks(a):
        ap = jnp.pad(to_res(a), ((0, 0), (0, 0), (rad, u_pad - n_res + rad), (0, 0), (0, 0)))
        ap = ap.reshape(bsz, dilation, n_blk + 2, rad, nh, dh)
        return jnp.concatenate([ap[:, :, :-2], ap[:, :, 1:-1], ap[:, :, 2:]], axis=3)

    kb, vb = key_blocks(k), key_blocks(v)
    s = jnp.einsum('brnqhc,brnkhc->brnhqk', qr, kb) * (dh ** -0.5)
    u_q = jnp.arange(u_pad).reshape(n_blk, rad)[:, :, None]
    u_k = (jnp.arange(n_blk)[:, None] * rad - rad + jnp.arange(3 * rad)[None, :])[:, None, :]
    dist = jnp.abs(u_q - u_k)
    valid = (dist <= rad) & (u_k >= 0) & (u_k < n_res)
    bias = -(slopes[None, :, None, None] * (dilation * dist).astype(F32)[:, None])
    s = jnp.where(valid[:, None], s + bias, NEG_INF)
    lse = jax.nn.logsumexp(s, axis=-1)
    p = jnp.exp(s - lse[..., None])
    o = jnp.einsum('brnhqk,brnkhc->brnqhc', p, vb)
    o = o.reshape(bsz, dilation, u_pad, nh, dh)[:, :, :n_res].transpose(0, 2, 1, 3, 4).reshape(bsz, seqlen, nh, dh)
    lse = lse.transpose(0, 1, 2, 4, 3).reshape(bsz, dilation, u_pad, nh)[:, :, :n_res]
    lse = lse.transpose(0, 2, 1, 3).reshape(bsz, seqlen, nh)
    return o, lse


def mixer_block(x, lw):
    t = x.shape[0]
    xb = x.astype(BF16)
    h_gla = matmul(xb, lw['w_gla'], F32)
    h_ra = matmul(xb, lw['w_ra'], F32, tn=128)
    h_att = matmul(xb, lw['w_att'], F32)
    h_conv = matmul(xb, lw['w_conv'], F32)

    qa, ka, va, ga = jnp.split(h_gla[None], (GLA_QK, 2 * GLA_QK, 2 * GLA_QK + GLA_V), axis=-1)
    qa = qa.reshape(1, t, GLA_HEADS, GLA_DK) * (GLA_DK ** -0.5)
    ka = ka.reshape(1, t, GLA_HEADS, GLA_DK)
    va = va.reshape(1, t, GLA_HEADS, GLA_DV)
    ra = h_ra[None, :, :2 * GLA_RANK].reshape(1, t, 2, GLA_RANK)
    z = jnp.einsum('btjr,jrk->btjk', ra, lw['w_gla_gate']) + lw['b_gla_gate']
    log_a = (jax.nn.log_sigmoid(z) / GLA_TEMP).reshape(1, t, 2, GLA_HEADS, GLA_DK)
    o_fwd = gla_chunked(qa, ka, va, log_a[:, :, 0])
    flip = lambda a: jnp.flip(a, axis=1)
    o_bwd = flip(gla_chunked(flip(qa), flip(ka), flip(va), flip(log_a[:, :, 1])))
    o_a = head_rms_norm(o_fwd + o_bwd, lw['gla_norm_g']) * jax.nn.silu(ga.reshape(1, t, GLA_HEADS, GLA_DV))
    o_a = o_a.reshape(t, GLA_V).astype(BF16)

    grp_shape = (1, t, N_ATT_GROUPS, ATT_HEADS_PER_GROUP, ATT_HEAD_DIM)
    qb, kb, vb = [a.reshape(grp_shape) for a in jnp.split(h_att[None], 3, axis=-1)]
    slopes = alibi_slopes(N_ATT_GROUPS * ATT_HEADS_PER_GROUP).reshape(N_ATT_GROUPS, ATT_HEADS_PER_GROUP)
    outs, lses = [], []
    for g, (win, dil) in enumerate(ATT_GROUPS):
        o_g, l_g = dilated_band_attention(qb[:, :, g], kb[:, :, g], vb[:, :, g], slopes[g], dil, win)
        outs.append(o_g)
        lses.append(l_g)
    o_stack = jnp.stack(outs, axis=2)
    wts = jax.nn.softmax(jnp.stack(lses, axis=2), axis=2)
    o_b = jnp.sum(wts[..., None] * o_stack, axis=2).reshape(t, ATT_OUT).astype(BF16)

    u_val, u_gate = jnp.split(h_conv[None], 2, axis=-1)
    u = u_val * jax.nn.sigmoid(u_gate)
    u = lax.conv_general_dilated(u, lw['conv_w'].reshape(CONV_WIDTH, 1, CONV_CH),
                                 window_strides=(1,), padding=[(CONV_WIDTH // 2, CONV_WIDTH // 2)],
                                 dimension_numbers=('NWC', 'WIO', 'NWC'), feature_group_count=CONV_CH) + lw['conv_b']
    u = jax.nn.silu(layer_norm(u, lw['conv_ln_g'], lw['conv_ln_b']))
    o_c = u.reshape(t, CONV_CH).astype(BF16)

    merged = gated_merge(xb, o_a, o_b, o_c, lw['w_gate3'], lw['b_gate3'], lw['w_branch'])
    return matmul(merged, lw['w_out'], F32)


def expert_choice_moe(x, lw):
    n_tok = x.shape[0]
    cap = max(1, EC_FACTOR * n_tok // N_EXPERTS)
    xb = x.astype(BF16)
    logits = matmul(xb, lw['w_router'], F32, tn=128)[:, :N_EXPERTS]
    probs = jax.nn.softmax(logits, axis=-1)
    gate, idx = lax.top_k(probs.T, cap)
    xe = jnp.take(xb, idx, axis=0)
    ye = expert_ffn(xe, gate[..., None], lw['w_e1'], lw['w_e3'], lw['w_e2'])
    return jnp.zeros_like(x).at[idx.reshape(-1)].add(ye.reshape(-1, D_MODEL))


def encoder_trunk(x, layers):
    alpha = (2.0 * DEPTH) ** 0.25
    x = x[0]
    for lw in layers:
        y = mixer_block(x, lw)
        x = layer_norm(alpha * x + y, lw['ln1_g'], lw['ln1_b'])
        m = expert_choice_moe(x, lw)
        x = layer_norm(alpha * x + m, lw['ln2_g'], lw['ln2_b'])
    return x[None]


def _prep_layer(l, w_in, w_gla_gate, b_gla_gate, gla_norm_g, conv_w, conv_b, conv_ln_g, conv_ln_b,
                w_branch, w_gate, b_gate, w_out, ln1_g, ln1_b, w_router, w_e1, w_e3, w_e2, ln2_g, ln2_b):
    o_ra, o_att, o_conv = IN_OFFSETS[3], IN_OFFSETS[4], IN_OFFSETS[7]
    wi = w_in[l]
    return {
        'w_gla': wi[:, :o_ra].astype(BF16),
        'w_ra': jnp.pad(wi[:, o_ra:o_att], ((0, 0), (0, 128 - 2 * GLA_RANK))).astype(BF16),
        'w_att': wi[:, o_att:o_conv].astype(BF16),
        'w_conv': wi[:, o_conv:].astype(BF16),
        'w_gla_gate': w_gla_gate[l], 'b_gla_gate': b_gla_gate[l], 'gla_norm_g': gla_norm_g[l],
        'conv_w': conv_w[l], 'conv_b': conv_b[l], 'conv_ln_g': conv_ln_g[l], 'conv_ln_b': conv_ln_b[l],
        'w_branch': w_branch[l].astype(BF16),
        'w_gate3': w_gate[l].reshape(D_MODEL, N_BRANCH, D_MODEL).transpose(1, 0, 2).astype(BF16),
        'b_gate3': b_gate[l].reshape(N_BRANCH, 1, D_MODEL),
        'w_out': w_out[l].astype(BF16),
        'ln1_g': ln1_g[l], 'ln1_b': ln1_b[l],
        'w_router': jnp.pad(w_router[l], ((0, 0), (0, 128 - N_EXPERTS))).astype(BF16),
        'w_e1': w_e1[l].astype(BF16), 'w_e3': w_e3[l].astype(BF16), 'w_e2': w_e2[l].astype(BF16),
        'ln2_g': ln2_g[l], 'ln2_b': ln2_b[l],
    }


def kernel(x_prompt, x_sample, w_in, w_gla_gate, b_gla_gate, gla_norm_g, conv_w, conv_b, conv_ln_g, conv_ln_b, w_branch, w_gate, b_gate, w_out, ln1_g, ln1_b, w_router, w_e1, w_e3, w_e2, ln2_g, ln2_b):
    params = (w_in, w_gla_gate, b_gla_gate, gla_norm_g, conv_w, conv_b, conv_ln_g, conv_ln_b,
              w_branch, w_gate, b_gate, w_out, ln1_g, ln1_b, w_router, w_e1, w_e3, w_e2, ln2_g, ln2_b)
    layers = [_prep_layer(l, *params) for l in range(DEPTH)]
    return (encoder_trunk(x_prompt, layers), encoder_trunk(x_sample, layers))
```

```python
import functools

import jax
import jax.numpy as jnp
import numpy as np
from jax import lax
from jax.experimental import pallas as pl
from jax.experimental.pallas import tpu as pltpu

D_MODEL = 4096
DEPTH = 4
GLA_HEADS = 4
GLA_DK = 128
GLA_DV = 256
GLA_RANK = 16
GLA_TEMP = 16.0
GLA_CHUNK = 64
ATT_GROUPS = ((128, 1), (512, 4), (2048, 16))
ATT_HEADS_PER_GROUP = 8
ATT_HEAD_DIM = 128
CONV_CH = 1024
CONV_WIDTH = 31
N_BRANCH = 3
BRANCH_WIDTH = 1024
N_EXPERTS = 16
EXPERT_FF = 2048
EC_FACTOR = 2
LN_EPS = 1e-5
NEG_INF = -1e30

GLA_QK = GLA_HEADS * GLA_DK
GLA_V = GLA_HEADS * GLA_DV
N_ATT_GROUPS = len(ATT_GROUPS)
ATT_QKV = N_ATT_GROUPS * ATT_HEADS_PER_GROUP * ATT_HEAD_DIM
ATT_OUT = ATT_HEADS_PER_GROUP * ATT_HEAD_DIM
IN_WIDTHS = (GLA_QK, GLA_QK, GLA_V, GLA_V, 2 * GLA_RANK, ATT_QKV, ATT_QKV, ATT_QKV, 2 * CONV_CH)
IN_OFFSETS = tuple(int(v) for v in np.cumsum(IN_WIDTHS)[:-1])
ALPHA = (2.0 * DEPTH) ** 0.25

V7X_VMEM_LIMIT_BYTES = 56 * 1024 * 1024
LANES = 128
BF16_SUBLANE_TILE = 16

BF16 = jnp.bfloat16
F32 = jnp.float32

_NT = (((1,), (1,)), ((), ()))
_TN = (((0,), (0,)), ((), ()))


def _cparams(sem):
    return pltpu.CompilerParams(dimension_semantics=sem, vmem_limit_bytes=V7X_VMEM_LIMIT_BYTES)


def _mm_body(x_ref, w_ref, o_ref):
    o_ref[...] = jnp.dot(x_ref[...], w_ref[...], preferred_element_type=F32).astype(o_ref.dtype)


def matmul(x, w, out_dtype, tm=1024, tn=1024):
    m, k = x.shape
    n = w.shape[1]
    tm, tn = min(tm, m), min(tn, n)
    assert m % tm == 0 and n % tn == 0
    return pl.pallas_call(
        _mm_body,
        grid=(m // tm, n // tn),
        in_specs=[pl.BlockSpec((tm, k), lambda i, j: (i, 0)),
                  pl.BlockSpec((k, tn), lambda i, j: (0, j))],
        out_specs=pl.BlockSpec((tm, tn), lambda i, j: (i, j)),
        out_shape=jax.ShapeDtypeStruct((m, n), out_dtype),
        compiler_params=_cparams(("parallel", "parallel")),
    )(x, w)


GLA_ROWS = 256


def _log_sigmoid(z):
    return jnp.minimum(z, 0.0) - jnp.log(1.0 + jnp.exp(-jnp.abs(z)))


def _gla_body(*refs, reverse, finalize):
    if finalize:
        q_ref, k_ref, v_ref, g_ref, ra_ref, wz_ref, bz_ref, ng_ref, ob_ref, o_ref, st_ref = refs
    else:
        q_ref, k_ref, v_ref, ra_ref, wz_ref, bz_ref, o_ref, st_ref = refs

    @pl.when(pl.program_id(0) == 0)
    def _():
        st_ref[...] = jnp.zeros_like(st_ref)

    z = jnp.dot(ra_ref[...], wz_ref[...], preferred_element_type=F32) + bz_ref[...]
    log_a = _log_sigmoid(z) / GLA_TEMP
    ri = lax.broadcasted_iota(jnp.int32, (GLA_CHUNK, GLA_CHUNK), 0)
    ci = lax.broadcasted_iota(jnp.int32, (GLA_CHUNK, GLA_CHUNK), 1)
    tri = (ci >= ri) if reverse else (ci <= ri)
    tri_b = tri.astype(BF16)
    n_chunk = q_ref.shape[0] // GLA_CHUNK
    order = range(n_chunk - 1, -1, -1) if reverse else range(n_chunk)
    for c in order:
        rows = slice(c * GLA_CHUNK, (c + 1) * GLA_CHUNK)
        la = log_a[rows]
        la_hi = la.astype(BF16)
        la_lo = (la - la_hi.astype(F32)).astype(BF16)
        bcum = (jnp.dot(tri_b, la_hi, preferred_element_type=F32)
                + jnp.dot(tri_b, la_lo, preferred_element_type=F32))
        b_last = bcum[0:1] if reverse else bcum[GLA_CHUNK - 1:GLA_CHUNK]
        e_pos = jnp.exp(bcum)
        e_neg = jnp.exp(-bcum)
        e_rem = jnp.exp(b_last - bcum)
        decay = jnp.exp(b_last)
        for h in range(GLA_HEADS):
            hk = slice(h * GLA_DK, (h + 1) * GLA_DK)
            hv = slice(h * GLA_DV, (h + 1) * GLA_DV)
            q = q_ref[rows, hk].astype(F32) * (GLA_DK ** -0.5)
            k = k_ref[rows, hk].astype(F32)
            v = v_ref[rows, hv]
            q_e = (q * e_pos[:, hk]).astype(BF16)
            k_e = (k * e_neg[:, hk]).astype(BF16)
            k_s = (k * e_rem[:, hk]).astype(BF16)
            attn = lax.dot_general(q_e, k_e, _NT, preferred_element_type=F32)
            attn = jnp.where(tri, attn, 0.0).astype(BF16)
            st = st_ref[h]
            o = (jnp.dot(attn, v, preferred_element_type=F32)
                 + lax.dot_general(q_e, st.astype(BF16), _NT, preferred_element_type=F32))
            d_state = lax.dot_general(v, k_s, _TN, preferred_element_type=F32)
            st_ref[h] = decay[:, hk] * st + d_state
            if finalize:
                o = o + ob_ref[rows, hv]
                o = o * lax.rsqrt(jnp.mean(jnp.square(o), axis=-1, keepdims=True) + LN_EPS) * ng_ref[...]
                g = g_ref[rows, hv].astype(F32)
                o = o * (g * jax.nn.sigmoid(g))
            o_ref[rows, hv] = o.astype(o_ref.dtype)


def gla_pass(h_gla, h_ra, wz, bz, norm_g=None, o_other=None):
    t = h_gla.shape[0]
    finalize = o_other is not None
    reverse = not finalize
    nb = t // GLA_ROWS
    blk = (lambda n: nb - 1 - n) if reverse else (lambda n: n)
    qk_blocks = GLA_QK // 512
    in_specs = [pl.BlockSpec((GLA_ROWS, GLA_QK), lambda n: (blk(n), 0)),
                pl.BlockSpec((GLA_ROWS, GLA_QK), lambda n: (blk(n), 1)),
                pl.BlockSpec((GLA_ROWS, GLA_V), lambda n: (blk(n), 1))]
    args = [h_gla, h_gla, h_gla]
    if finalize:
        in_specs.append(pl.BlockSpec((GLA_ROWS, GLA_V), lambda n: (blk(n), 2)))
        args.append(h_gla)
    in_specs += [pl.BlockSpec((GLA_ROWS, LANES), lambda n: (blk(n), 0)),
                 pl.BlockSpec((LANES, GLA_QK), lambda n: (0, 0)),
                 pl.BlockSpec((1, GLA_QK), lambda n: (0, 0))]
    args += [h_ra, wz, bz]
    if finalize:
        in_specs += [pl.BlockSpec((1, GLA_DV), lambda n: (0, 0)),
                     pl.BlockSpec((GLA_ROWS, GLA_V), lambda n: (blk(n), 0))]
        args += [norm_g, o_other]
    assert qk_blocks == 1
    return pl.pallas_call(
        functools.partial(_gla_body, reverse=reverse, finalize=finalize),
        grid=(nb,),
        in_specs=in_specs,
        out_specs=pl.BlockSpec((GLA_ROWS, GLA_V), lambda n: (blk(n), 0)),
        out_shape=jax.ShapeDtypeStruct((t, GLA_V), BF16 if finalize else F32),
        scratch_shapes=[pltpu.VMEM((GLA_HEADS, GLA_DV, GLA_DK), F32)],
        compiler_params=_cparams(("arbitrary",)),
    )(*args)


ATT_RAD = 64
ATT_QROWS = 256
ATT_SUB = 128


def _attn_body(q_ref, kp_ref, kc_ref, kn_ref, vp_ref, vc_ref, vn_ref, o_ref, lse_ref, *, dilation, n_res, slopes):
    m = pl.program_id(1)
    n_sub = ATT_QROWS // ATT_SUB
    n_keys = ATT_SUB + 2 * ATT_RAD
    qi = lax.broadcasted_iota(jnp.int32, (ATT_SUB, n_keys), 0)
    kj = lax.broadcasted_iota(jnp.int32, (ATT_SUB, n_keys), 1)
    dist = jnp.abs(kj - ATT_RAD - qi)
    dist_f = (dilation * dist).astype(F32)
    valid = []
    for j in range(n_sub):
        key_row = m * ATT_QROWS + j * ATT_SUB - ATT_RAD + kj
        valid.append((dist <= ATT_RAD) & (key_row >= 0) & (key_row < n_res))
    lane = lax.broadcasted_iota(jnp.int32, (ATT_SUB, LANES), 1)
    lse_tiles = [jnp.zeros((ATT_SUB, LANES), F32) for _ in range(n_sub)]
    for h in range(ATT_HEADS_PER_GROUP):
        hs = slice(h * ATT_HEAD_DIM, (h + 1) * ATT_HEAD_DIM)
        k_all = jnp.concatenate([kp_ref[:, hs], kc_ref[:, hs], kn_ref[:, hs]], axis=0)
        v_all = jnp.concatenate([vp_ref[:, hs], vc_ref[:, hs], vn_ref[:, hs]], axis=0)
        for j in range(n_sub):
            q = q_ref[j * ATT_SUB:(j + 1) * ATT_SUB, hs]
            k = k_all[j * ATT_SUB:j * ATT_SUB + n_keys]
            v = v_all[j * ATT_SUB:j * ATT_SUB + n_keys]
            s = lax.dot_general(q, k, _NT, preferred_element_type=F32) * (ATT_HEAD_DIM ** -0.5)
            s = jnp.where(valid[j], s - slopes[h] * dist_f, NEG_INF)
            mx = jnp.max(s, axis=-1, keepdims=True)
            p = jnp.exp(s - mx)
            l = jnp.sum(p, axis=-1, keepdims=True)
            o = jnp.dot(p.astype(BF16), v, preferred_element_type=F32) / l
            o_ref[j * ATT_SUB:(j + 1) * ATT_SUB, hs] = o.astype(o_ref.dtype)
            lse_tiles[j] = jnp.where(lane == h, mx + jnp.log(l), lse_tiles[j])
    for j in range(n_sub):
        lse_ref[j * ATT_SUB:(j + 1) * ATT_SUB, :] = lse_tiles[j]


def dilated_attention_group(h_att, g):
    window, dilation = ATT_GROUPS[g]
    assert window // (2 * dilation) == ATT_RAD
    t = h_att.shape[0]
    n_res = t // dilation
    assert n_res % ATT_QROWS == 0
    nblk = n_res // ATT_QROWS
    halo_per_blk = ATT_QROWS // ATT_RAD
    n_halo = n_res // ATT_RAD
    width = ATT_HEADS_PER_GROUP * ATT_HEAD_DIM
    cols_per_res = 3 * N_ATT_GROUPS
    view = h_att.reshape(n_res, dilation * 3 * ATT_QKV)
    n_total = N_ATT_GROUPS * ATT_HEADS_PER_GROUP
    slopes = tuple(float(np.float32(2.0 ** (-8.0 * (g * ATT_HEADS_PER_GROUP + h + 1) / n_total)))
                   for h in range(ATT_HEADS_PER_GROUP))

    def cur(part):
        return pl.BlockSpec((ATT_QROWS, width), lambda r, m: (m, r * cols_per_res + part * N_ATT_GROUPS + g))

    def prev(part):
        return pl.BlockSpec((ATT_RAD, width), lambda r, m: (jnp.maximum(m * halo_per_blk - 1, 0),
                                                            r * cols_per_res + part * N_ATT_GROUPS + g))

    def nxt(part):
        return pl.BlockSpec((ATT_RAD, width), lambda r, m: (jnp.minimum((m + 1) * halo_per_blk, n_halo - 1),
                                                            r * cols_per_res + part * N_ATT_GROUPS + g))

    o, lse = pl.pallas_call(
        functools.partial(_attn_body, dilation=dilation, n_res=n_res, slopes=slopes),
        grid=(dilation, nblk),
        in_specs=[cur(0), prev(1), cur(1), nxt(1), prev(2), cur(2), nxt(2)],
        out_specs=[pl.BlockSpec((ATT_QROWS, width), lambda r, m: (m, r)),
                   pl.BlockSpec((ATT_QROWS, LANES), lambda r, m: (m, r))],
        out_shape=[jax.ShapeDtypeStruct((n_res, dilation * width), BF16),
                   jax.ShapeDtypeStruct((n_res, dilation * LANES), F32)],
        compiler_params=_cparams(("parallel", "parallel")),
    )(view, view, view, view, view, view, view)
    return o.reshape(t, width), lse.reshape(t, LANES)


def _attn_merge_body(o0_ref, o1_ref, o2_ref, l0_ref, l1_ref, l2_ref, o_ref):
    l0, l1, l2 = l0_ref[...], l1_ref[...], l2_ref[...]
    mx = jnp.maximum(jnp.maximum(l0, l1), l2)
    e0, e1, e2 = jnp.exp(l0 - mx), jnp.exp(l1 - mx), jnp.exp(l2 - mx)
    inv = 1.0 / (e0 + e1 + e2)
    w0, w1, w2 = e0 * inv, e1 * inv, e2 * inv
    for h in range(ATT_HEADS_PER_GROUP):
        hs = slice(h * ATT_HEAD_DIM, (h + 1) * ATT_HEAD_DIM)
        acc = (w0[:, h:h + 1] * o0_ref[:, hs].astype(F32) + w1[:, h:h + 1] * o1_ref[:, hs].astype(F32)
               + w2[:, h:h + 1] * o2_ref[:, hs].astype(F32))
        o_ref[:, hs] = acc.astype(o_ref.dtype)


def attention_merge(outs, lses, tm=512):
    t, width = outs[0].shape
    o_spec = pl.BlockSpec((tm, width), lambda i: (i, 0))
    l_spec = pl.BlockSpec((tm, LANES), lambda i: (i, 0))
    return pl.pallas_call(
        _attn_merge_body,
        grid=(t // tm,),
        in_specs=[o_spec] * 3 + [l_spec] * 3,
        out_specs=o_spec,
        out_shape=jax.ShapeDtypeStruct((t, width), BF16),
        compiler_params=_cparams(("parallel",)),
    )(*outs, *lses)


CONV_ROWS = 256
CONV_HALO = 16


def _conv_body(vp_ref, vc_ref, vn_ref, gp_ref, gc_ref, gn_ref, cw_ref, cb_ref, lg_ref, lb_ref, o_ref, u_ref):
    m = pl.program_id(0)
    last = pl.num_programs(0) - 1

    def glu(v_ref, g_ref):
        return v_ref[...].astype(F32) * jax.nn.sigmoid(g_ref[...].astype(F32))

    u_ref[0:CONV_HALO, :] = jnp.where(m > 0, glu(vp_ref, gp_ref), 0.0)
    u_ref[CONV_HALO:CONV_HALO + CONV_ROWS, :] = glu(vc_ref, gc_ref)
    u_ref[CONV_HALO + CONV_ROWS:, :] = jnp.where(m < last, glu(vn_ref, gn_ref), 0.0)
    base = CONV_HALO - CONV_WIDTH // 2
    acc = None
    for w in range(CONV_WIDTH):
        term = u_ref[base + w:base + w + CONV_ROWS, :] * cw_ref[w:w + 1, :]
        acc = term if acc is None else acc + term
    acc = acc + cb_ref[...]
    mu = jnp.mean(acc, axis=-1, keepdims=True)
    xc = acc - mu
    var = jnp.mean(jnp.square(xc), axis=-1, keepdims=True)
    y = xc * lax.rsqrt(var + LN_EPS) * lg_ref[...] + lb_ref[...]
    o_ref[...] = (y * jax.nn.sigmoid(y)).astype(o_ref.dtype)


def conv_module(h_conv, conv_w, conv_b, ln_g, ln_b):
    t = h_conv.shape[0]
    nb = t // CONV_ROWS
    halo_per_blk = CONV_ROWS // CONV_HALO
    n_halo = t // CONV_HALO

    def cur(c):
        return pl.BlockSpec((CONV_ROWS, CONV_CH), lambda m: (m, c))

    def prev(c):
        return pl.BlockSpec((CONV_HALO, CONV_CH), lambda m: (jnp.maximum(m * halo_per_blk - 1, 0), c))

    def nxt(c):
        return pl.BlockSpec((CONV_HALO, CONV_CH), lambda m: (jnp.minimum((m + 1) * halo_per_blk, n_halo - 1), c))

    vec = pl.BlockSpec((1, CONV_CH), lambda m: (0, 0))
    return pl.pallas_call(
        _conv_body,
        grid=(nb,),
        in_specs=[prev(0), cur(0), nxt(0), prev(1), cur(1), nxt(1),
                  pl.BlockSpec((CONV_WIDTH, CONV_CH), lambda m: (0, 0)), vec, vec, vec],
        out_specs=pl.BlockSpec((CONV_ROWS, CONV_CH), lambda m: (m, 0)),
        out_shape=jax.ShapeDtypeStruct((t, CONV_CH), BF16),
        scratch_shapes=[pltpu.VMEM((CONV_ROWS + 2 * CONV_HALO, CONV_CH), F32)],
        compiler_params=_cparams(("parallel",)),
    )(h_conv, h_conv, h_conv, h_conv, h_conv, h_conv, conv_w, conv_b, ln_g, ln_b)


def _merge_body(x_ref, oa_ref, ob_ref, oc_ref, wg_ref, bg_ref, wb_ref, o_ref):
    x = x_ref[...]
    acc = None
    for i, o_i in enumerate((oa_ref, ob_ref, oc_ref)):
        g = jnp.dot(x, wg_ref[i], preferred_element_type=F32) + bg_ref[i]
        br = jnp.dot(o_i[...], wb_ref[i], preferred_element_type=F32)
        term = jax.nn.sigmoid(g) * br
        acc = term if acc is None else acc + term
    o_ref[...] = acc.astype(o_ref.dtype)


def gated_merge(x, o_a, o_b, o_c, w_gate3, b_gate3, w_branch, tm=512, tn=512):
    t, d = x.shape
    bw = o_a.shape[1]
    o_spec = pl.BlockSpec((tm, bw), lambda i, j: (i, 0))
    return pl.pallas_call(
        _merge_body,
        grid=(t // tm, d // tn),
        in_specs=[pl.BlockSpec((tm, d), lambda i, j: (i, 0)), o_spec, o_spec, o_spec,
                  pl.BlockSpec((N_BRANCH, d, tn), lambda i, j: (0, 0, j)),
                  pl.BlockSpec((N_BRANCH, 1, tn), lambda i, j: (0, 0, j)),
                  pl.BlockSpec((N_BRANCH, bw, tn), lambda i, j: (0, 0, j))],
        out_specs=pl.BlockSpec((tm, tn), lambda i, j: (i, j)),
        out_shape=jax.ShapeDtypeStruct((t, d), BF16),
        compiler_params=_cparams(("parallel", "parallel")),
    )(x, o_a, o_b, o_c, w_gate3, b_gate3, w_branch)


def _add_ln_body(*refs, res_scale, out_scale, with_router):
    if with_router:
        r_ref, y_ref, g_ref, b_ref, wr_ref, res_ref, xb_ref, lg_ref = refs
    else:
        r_ref, y_ref, g_ref, b_ref, res_ref, xb_ref = refs
    s = r_ref[...] * res_scale + y_ref[...] if res_scale != 1.0 else r_ref[...] + y_ref[...]
    mu = jnp.mean(s, axis=-1, keepdims=True)
    xc = s - mu
    var = jnp.mean(jnp.square(xc), axis=-1, keepdims=True)
    xn = xc * lax.rsqrt(var + LN_EPS) * g_ref[...] + b_ref[...]
    res_ref[...] = xn * out_scale if out_scale != 1.0 else xn
    xb = xn.astype(BF16)
    xb_ref[...] = xb
    if with_router:
        lg_ref[...] = jnp.dot(xb, wr_ref[...], preferred_element_type=F32)


def add_layer_norm(res, y, g, b, res_scale, out_scale, w_router=None, tm=256):
    t, d = res.shape
    with_router = w_router is not None
    row = pl.BlockSpec((tm, d), lambda i: (i, 0))
    vec = pl.BlockSpec((1, d), lambda i: (0, 0))
    in_specs = [row, row, vec, vec]
    args = [res, y, g, b]
    out_specs = [row, row]
    out_shape = [jax.ShapeDtypeStruct((t, d), F32), jax.ShapeDtypeStruct((t, d), BF16)]
    if with_router:
        in_specs.append(pl.BlockSpec((d, LANES), lambda i: (0, 0)))
        args.append(w_router)
        out_specs.append(pl.BlockSpec((tm, LANES), lambda i: (i, 0)))
        out_shape.append(jax.ShapeDtypeStruct((t, LANES), F32))
    return pl.pallas_call(
        functools.partial(_add_ln_body, res_scale=res_scale, out_scale=out_scale, with_router=with_router),
        grid=(t // tm,),
        in_specs=in_specs,
        out_specs=out_specs,
        out_shape=out_shape,
        compiler_params=_cparams(("parallel",)),
    )(*args)


def _ffn_body(xe_ref, g_ref, w1_ref, w3_ref, w2_ref, o_ref):
    f = pl.program_id(2)
    xe = xe_ref[0]
    h1 = jnp.dot(xe, w1_ref[0], preferred_element_type=F32)
    h3 = jnp.dot(xe, w3_ref[0], preferred_element_type=F32)
    hid = (h1 * jax.nn.sigmoid(h1) * h3).astype(BF16)
    part = jnp.dot(hid, w2_ref[0], preferred_element_type=F32)

    @pl.when(f == 0)
    def _():
        o_ref[0] = part

    @pl.when(f > 0)
    def _():
        o_ref[0] += part

    @pl.when(f == pl.num_programs(2) - 1)
    def _():
        o_ref[0] = o_ref[0] * g_ref[0]


def expert_ffn(xe, gate, w1, w3, w2, tm=512, tf=256):
    e, c, d = xe.shape
    ff = w1.shape[2]
    tm = min(tm, c)
    return pl.pallas_call(
        _ffn_body,
        grid=(e, c // tm, ff // tf),
        in_specs=[pl.BlockSpec((1, tm, d), lambda a, i, f: (a, i, 0)),
                  pl.BlockSpec((1, tm, 1), lambda a, i, f: (a, i, 0)),
                  pl.BlockSpec((1, d, tf), lambda a, i, f: (a, 0, f)),
                  pl.BlockSpec((1, d, tf), lambda a, i, f: (a, 0, f)),
                  pl.BlockSpec((1, tf, d), lambda a, i, f: (a, f, 0))],
        out_specs=pl.BlockSpec((1, tm, d), lambda a, i, f: (a, i, 0)),
        out_shape=jax.ShapeDtypeStruct((e, c, d), F32),
        compiler_params=_cparams(("parallel", "parallel", "arbitrary")),
    )(xe, gate, w1, w3, w2)


def mixer_block(xb, lw):
    h_gla = matmul(xb, lw['w_gla'], BF16)
    h_ra = matmul(xb, lw['w_ra'], BF16, tn=LANES)
    h_att = matmul(xb, lw['w_att'], BF16)
    h_conv = matmul(xb, lw['w_conv'], BF16)

    o_bwd = gla_pass(h_gla, h_ra, lw['wz'][1], lw['bz'][1])
    o_a = gla_pass(h_gla, h_ra, lw['wz'][0], lw['bz'][0], lw['gla_norm_g'], o_bwd)

    outs, lses = zip(*[dilated_attention_group(h_att, g) for g in range(N_ATT_GROUPS)])
    o_b = attention_merge(outs, lses)

    o_c = conv_module(h_conv, lw['conv_w'], lw['conv_b'], lw['conv_ln_g'], lw['conv_ln_b'])

    merged = gated_merge(xb, o_a, o_b, o_c, lw['w_gate3'], lw['b_gate3'], lw['w_branch'])
    return matmul(merged, lw['w_out'], F32)


def expert_choice_moe(xb, logits, lw):
    n_tok = xb.shape[0]
    cap = max(1, EC_FACTOR * n_tok // N_EXPERTS)
    probs = jax.nn.softmax(logits[:, :N_EXPERTS], axis=-1)
    gate, idx = lax.top_k(probs.T, cap)
    xe = jnp.take(xb, idx, axis=0)
    ye = expert_ffn(xe, gate[..., None], lw['w_e1'], lw['w_e3'], lw['w_e2'])
    return jnp.zeros((n_tok, D_MODEL), F32).at[idx.reshape(-1)].add(ye.reshape(-1, D_MODEL))


def encoder_trunk(x, layers):
    x = x[0]
    res, res_scale = x, ALPHA
    xb = x.astype(BF16)
    for li, lw in enumerate(layers):
        y = mixer_block(xb, lw)
        res, xb, logits = add_layer_norm(res, y, lw['ln1_g'], lw['ln1_b'], res_scale, ALPHA, lw['w_router'])
        m = expert_choice_moe(xb, logits, lw)
        out_scale = ALPHA if li + 1 < len(layers) else 1.0
        res, xb = add_layer_norm(res, m, lw['ln2_g'], lw['ln2_b'], 1.0, out_scale)
        res_scale = 1.0
    return res[None]


def _prep_layer(l, w_in, w_gla_gate, b_gla_gate, gla_norm_g, conv_w, conv_b, conv_ln_g, conv_ln_b,
                w_branch, w_gate, b_gate, w_out, ln1_g, ln1_b, w_router, w_e1, w_e3, w_e2, ln2_g, ln2_b):
    o_ra, o_att, o_conv = IN_OFFSETS[3], IN_OFFSETS[4], IN_OFFSETS[7]
    wi = w_in[l]
    wz = jnp.zeros((2, LANES, GLA_QK), F32)
    wz = wz.at[0, :GLA_RANK].set(w_gla_gate[l, 0]).at[1, GLA_RANK:2 * GLA_RANK].set(w_gla_gate[l, 1])
    row = lambda a: a.reshape(1, -1)
    return {
        'w_gla': wi[:, :o_ra].astype(BF16),
        'w_ra': jnp.pad(wi[:, o_ra:o_att], ((0, 0), (0, LANES - 2 * GLA_RANK))).astype(BF16),
        'w_att': wi[:, o_att:o_conv].astype(BF16),
        'w_conv': wi[:, o_conv:].astype(BF16),
        'wz': wz.astype(BF16), 'bz': b_gla_gate[l].reshape(2, 1, GLA_QK),
        'gla_norm_g': row(gla_norm_g[l]),
        'conv_w': conv_w[l], 'conv_b': row(conv_b[l]), 'conv_ln_g': row(conv_ln_g[l]), 'conv_ln_b': row(conv_ln_b[l]),
        'w_branch': w_branch[l].astype(BF16),
        'w_gate3': w_gate[l].reshape(D_MODEL, N_BRANCH, D_MODEL).transpose(1, 0, 2).astype(BF16),
        'b_gate3': b_gate[l].reshape(N_BRANCH, 1, D_MODEL),
        'w_out': w_out[l].astype(BF16),
        'ln1_g': row(ln1_g[l]), 'ln1_b': row(ln1_b[l]),
        'w_router': jnp.pad(w_router[l], ((0, 0), (0, LANES - N_EXPERTS))).astype(BF16),
        'w_e1': w_e1[l].astype(BF16), 'w_e3': w_e3[l].astype(BF16), 'w_e2': w_e2[l].astype(BF16),
        'ln2_g': row(ln2_g[l]), 'ln2_b': row(ln2_b[l]),
    }


def kernel(x_prompt, x_sample, w_in, w_gla_gate, b_gla_gate, gla_norm_g, conv_w, conv_b, conv_ln_g, conv_ln_b, w_branch, w_gate, b_gate, w_out, ln1_g, ln1_b, w_router, w_e1, w_e3, w_e2, ln2_g, ln2_b):
    params = (w_in, w_gla_gate, b_gla_gate, gla_norm_g, conv_w, conv_b, conv_ln_g, conv_ln_b,
              w_branch, w_gate, b_gate, w_out, ln1_g, ln1_b, w_router, w_e1, w_e3, w_e2, ln2_g, ln2_b)
    layers = [_prep_layer(l, *params) for l in range(DEPTH)]
    return (encoder_trunk(x_prompt, layers), encoder_trunk(x_sample, layers))
```

```python
import functools

import jax
import jax.numpy as jnp
import numpy as np
from jax import lax
from jax.experimental import pallas as pl
from jax.experimental.pallas import tpu as pltpu

D_MODEL = 4096
DEPTH = 4
GLA_HEADS = 4
GLA_DK = 128
GLA_DV = 256
GLA_RANK = 16
GLA_TEMP = 16.0
GLA_CHUNK = 64
ATT_GROUPS = ((128, 1), (512, 4), (2048, 16))
ATT_HEADS_PER_GROUP = 8
ATT_HEAD_DIM = 128
CONV_CH = 1024
CONV_WIDTH = 31
N_BRANCH = 3
BRANCH_WIDTH = 1024
N_EXPERTS = 16
EXPERT_FF = 2048
EC_FACTOR = 2
LN_EPS = 1e-5
NEG_INF = -1e30

GLA_QK = GLA_HEADS * GLA_DK
GLA_V = GLA_HEADS * GLA_DV
N_ATT_GROUPS = len(ATT_GROUPS)
ATT_QKV = N_ATT_GROUPS * ATT_HEADS_PER_GROUP * ATT_HEAD_DIM
ATT_OUT = ATT_HEADS_PER_GROUP * ATT_HEAD_DIM
IN_WIDTHS = (GLA_QK, GLA_QK, GLA_V, GLA_V, 2 * GLA_RANK, ATT_QKV, ATT_QKV, ATT_QKV, 2 * CONV_CH)
IN_OFFSETS = tuple(int(v) for v in np.cumsum(IN_WIDTHS)[:-1])
ALPHA = (2.0 * DEPTH) ** 0.25

V7X_VMEM_LIMIT_BYTES = 56 * 1024 * 1024
LANES = 128

BF16 = jnp.bfloat16
F32 = jnp.float32

_NT = (((1,), (1,)), ((), ()))
_TN = (((0,), (0,)), ((), ()))


def _cparams(sem):
    return pltpu.CompilerParams(dimension_semantics=sem, vmem_limit_bytes=V7X_VMEM_LIMIT_BYTES)


def _mm_body(x_ref, w_ref, o_ref):
    o_ref[...] = jnp.dot(x_ref[...], w_ref[...], preferred_element_type=F32).astype(o_ref.dtype)


def matmul(x, w, out_dtype, tm=1024, tn=1024):
    m, k = x.shape
    n = w.shape[1]
    tm, tn = min(tm, m), min(tn, n)
    assert m % tm == 0 and n % tn == 0
    return pl.pallas_call(
        _mm_body,
        grid=(m // tm, n // tn),
        in_specs=[pl.BlockSpec((tm, k), lambda i, j: (i, 0)),
                  pl.BlockSpec((k, tn), lambda i, j: (0, j))],
        out_specs=pl.BlockSpec((tm, tn), lambda i, j: (i, j)),
        out_shape=jax.ShapeDtypeStruct((m, n), out_dtype),
        compiler_params=_cparams(("parallel", "parallel")),
    )(x, w)


TOKEN_ROWS = 256
DILATIONS = tuple(d for _, d in ATT_GROUPS)


def _residue_permutation(n, dilation, transpose):
    per = n // dilation
    shift_per, shift_d = per.bit_length() - 1, dilation.bit_length() - 1
    i = lax.broadcasted_iota(jnp.int32, (n, n), 1 if transpose else 0)
    j = lax.broadcasted_iota(jnp.int32, (n, n), 0 if transpose else 1)
    tok = ((i & (per - 1)) << shift_d) | (i >> shift_per)
    return (tok == j).astype(BF16)


def _store_bf16_views(xb, xb_refs):
    for d, ref in zip(DILATIONS, xb_refs):
        if d == 1:
            ref[...] = xb
        else:
            perm = jnp.dot(_residue_permutation(TOKEN_ROWS, d, transpose=False), xb,
                           preferred_element_type=F32).astype(BF16)
            per = TOKEN_ROWS // d
            for r in range(d):
                ref[r] = perm[r * per:(r + 1) * per]


def _bf16_view_specs(t, d_model):
    specs, shapes = [], []
    for d in DILATIONS:
        if d == 1:
            specs.append(pl.BlockSpec((TOKEN_ROWS, d_model), lambda i: (i, 0)))
            shapes.append(jax.ShapeDtypeStruct((t, d_model), BF16))
        else:
            specs.append(pl.BlockSpec((d, TOKEN_ROWS // d, d_model), lambda i: (0, i, 0)))
            shapes.append(jax.ShapeDtypeStruct((d, t // d, d_model), BF16))
    return specs, shapes


def _cast_views_body(x_ref, *xb_refs):
    _store_bf16_views(x_ref[...].astype(BF16), xb_refs)


def cast_views(x):
    t, d_model = x.shape
    specs, shapes = _bf16_view_specs(t, d_model)
    return pl.pallas_call(
        _cast_views_body,
        grid=(t // TOKEN_ROWS,),
        in_specs=[pl.BlockSpec((TOKEN_ROWS, d_model), lambda i: (i, 0))],
        out_specs=specs,
        out_shape=shapes,
        compiler_params=_cparams(("parallel",)),
    )(x)


GLA_ROWS = 256


def _log_sigmoid(z):
    return jnp.minimum(z, 0.0) - jnp.log(1.0 + jnp.exp(-jnp.abs(z)))


def _gla_body(*refs, reverse, finalize):
    if finalize:
        q_ref, k_ref, v_ref, g_ref, ra_ref, wz_ref, bz_ref, ng_ref, ob_ref, o_ref, st_ref = refs
    else:
        q_ref, k_ref, v_ref, ra_ref, wz_ref, bz_ref, o_ref, st_ref = refs

    @pl.when(pl.program_id(0) == 0)
    def _():
        st_ref[...] = jnp.zeros_like(st_ref)

    z = jnp.dot(ra_ref[...], wz_ref[...], preferred_element_type=F32) + bz_ref[...]
    log_a = _log_sigmoid(z) / GLA_TEMP
    ri = lax.broadcasted_iota(jnp.int32, (GLA_CHUNK, GLA_CHUNK), 0)
    ci = lax.broadcasted_iota(jnp.int32, (GLA_CHUNK, GLA_CHUNK), 1)
    tri = (ci >= ri) if reverse else (ci <= ri)
    tri_b = tri.astype(BF16)
    n_chunk = q_ref.shape[0] // GLA_CHUNK
    order = range(n_chunk - 1, -1, -1) if reverse else range(n_chunk)
    for c in order:
        rows = slice(c * GLA_CHUNK, (c + 1) * GLA_CHUNK)
        la = log_a[rows]
        la_hi = la.astype(BF16)
        la_lo = (la - la_hi.astype(F32)).astype(BF16)
        bcum = (jnp.dot(tri_b, la_hi, preferred_element_type=F32)
                + jnp.dot(tri_b, la_lo, preferred_element_type=F32))
        b_last = bcum[0:1] if reverse else bcum[GLA_CHUNK - 1:GLA_CHUNK]
        e_pos = jnp.exp(bcum)
        e_neg = jnp.exp(-bcum)
        e_rem = jnp.exp(b_last - bcum)
        decay = jnp.exp(b_last)
        for h in range(GLA_HEADS):
            hk = slice(h * GLA_DK, (h + 1) * GLA_DK)
            hv = slice(h * GLA_DV, (h + 1) * GLA_DV)
            q = q_ref[rows, hk].astype(F32) * (GLA_DK ** -0.5)
            k = k_ref[rows, hk].astype(F32)
            v = v_ref[rows, hv]
            q_e = (q * e_pos[:, hk]).astype(BF16)
            k_e = (k * e_neg[:, hk]).astype(BF16)
            k_s = (k * e_rem[:, hk]).astype(BF16)
            attn = lax.dot_general(q_e, k_e, _NT, preferred_element_type=F32)
            attn = jnp.where(tri, attn, 0.0).astype(BF16)
            st = st_ref[h]
            o = (jnp.dot(attn, v, preferred_element_type=F32)
                 + lax.dot_general(q_e, st.astype(BF16), _NT, preferred_element_type=F32))
            d_state = lax.dot_general(v, k_s, _TN, preferred_element_type=F32)
            st_ref[h] = decay[:, hk] * st + d_state
            if finalize:
                o = o + ob_ref[rows, hv]
                o = o * lax.rsqrt(jnp.mean(jnp.square(o), axis=-1, keepdims=True) + LN_EPS) * ng_ref[...]
                g = g_ref[rows, hv].astype(F32)
                o = o * (g * jax.nn.sigmoid(g))
            o_ref[rows, hv] = o.astype(o_ref.dtype)


def gla_pass(h_gla, h_ra, wz, bz, norm_g=None, o_other=None):
    t = h_gla.shape[0]
    finalize = o_other is not None
    reverse = not finalize
    nb = t // GLA_ROWS
    blk = (lambda n: nb - 1 - n) if reverse else (lambda n: n)
    in_specs = [pl.BlockSpec((GLA_ROWS, GLA_QK), lambda n: (blk(n), 0)),
                pl.BlockSpec((GLA_ROWS, GLA_QK), lambda n: (blk(n), 1)),
                pl.BlockSpec((GLA_ROWS, GLA_V), lambda n: (blk(n), 1))]
    args = [h_gla, h_gla, h_gla]
    if finalize:
        in_specs.append(pl.BlockSpec((GLA_ROWS, GLA_V), lambda n: (blk(n), 2)))
        args.append(h_gla)
    in_specs += [pl.BlockSpec((GLA_ROWS, LANES), lambda n: (blk(n), 0)),
                 pl.BlockSpec((LANES, GLA_QK), lambda n: (0, 0)),
                 pl.BlockSpec((1, GLA_QK), lambda n: (0, 0))]
    args += [h_ra, wz, bz]
    if finalize:
        in_specs += [pl.BlockSpec((1, GLA_DV), lambda n: (0, 0)),
                     pl.BlockSpec((GLA_ROWS, GLA_V), lambda n: (blk(n), 0))]
        args += [norm_g, o_other]
    return pl.pallas_call(
        functools.partial(_gla_body, reverse=reverse, finalize=finalize),
        grid=(nb,),
        in_specs=in_specs,
        out_specs=pl.BlockSpec((GLA_ROWS, GLA_V), lambda n: (blk(n), 0)),
        out_shape=jax.ShapeDtypeStruct((t, GLA_V), BF16 if finalize else F32),
        scratch_shapes=[pltpu.VMEM((GLA_HEADS, GLA_DV, GLA_DK), F32)],
        compiler_params=_cparams(("arbitrary",)),
    )(*args)


ATT_RAD = 64
ATT_QROWS = 256
ATT_SUB = 128


def _attn_body(q_ref, kp_ref, kc_ref, kn_ref, vp_ref, vc_ref, vn_ref, o_ref, lse_ref, *, dilation, n_res, slopes):
    m = pl.program_id(1)
    n_sub = ATT_QROWS // ATT_SUB
    n_keys = ATT_SUB + 2 * ATT_RAD
    qi = lax.broadcasted_iota(jnp.int32, (ATT_SUB, n_keys), 0)
    kj = lax.broadcasted_iota(jnp.int32, (ATT_SUB, n_keys), 1)
    dist = jnp.abs(kj - ATT_RAD - qi)
    dist_f = (dilation * dist).astype(F32)
    valid = []
    for j in range(n_sub):
        key_row = m * ATT_QROWS + j * ATT_SUB - ATT_RAD + kj
        valid.append((dist <= ATT_RAD) & (key_row >= 0) & (key_row < n_res))
    lane = lax.broadcasted_iota(jnp.int32, (ATT_SUB, LANES), 1)
    lse_tiles = [jnp.zeros((ATT_SUB, LANES), F32) for _ in range(n_sub)]
    for h in range(ATT_HEADS_PER_GROUP):
        hs = slice(h * ATT_HEAD_DIM, (h + 1) * ATT_HEAD_DIM)
        k_all = jnp.concatenate([kp_ref[:, hs], kc_ref[:, hs], kn_ref[:, hs]], axis=0)
        v_all = jnp.concatenate([vp_ref[:, hs], vc_ref[:, hs], vn_ref[:, hs]], axis=0)
        for j in range(n_sub):
            q = q_ref[j * ATT_SUB:(j + 1) * ATT_SUB, hs]
            k = k_all[j * ATT_SUB:j * ATT_SUB + n_keys]
            v = v_all[j * ATT_SUB:j * ATT_SUB + n_keys]
            s = lax.dot_general(q, k, _NT, preferred_element_type=F32) * (ATT_HEAD_DIM ** -0.5)
            s = jnp.where(valid[j], s - slopes[h] * dist_f, NEG_INF)
            mx = jnp.max(s, axis=-1, keepdims=True)
            p = jnp.exp(s - mx)
            l = jnp.sum(p, axis=-1, keepdims=True)
            o = jnp.dot(p.astype(BF16), v, preferred_element_type=F32) / l
            o_ref[j * ATT_SUB:(j + 1) * ATT_SUB, hs] = o.astype(o_ref.dtype)
            lse_tiles[j] = jnp.where(lane == h, mx + jnp.log(l), lse_tiles[j])
    for j in range(n_sub):
        lse_ref[j * ATT_SUB:(j + 1) * ATT_SUB, :] = lse_tiles[j]


def dilated_attention_group(h_g, g):
    window, dilation = ATT_GROUPS[g]
    assert window // (2 * dilation) == ATT_RAD
    t = h_g.shape[0]
    n_res = t // dilation
    assert n_res % ATT_QROWS == 0
    nblk = n_res // ATT_QROWS
    halo_per_blk = ATT_QROWS // ATT_RAD
    n_halo = n_res // ATT_RAD
    width = ATT_HEADS_PER_GROUP * ATT_HEAD_DIM
    view = h_g.reshape(dilation, n_res, 3 * width)
    n_total = N_ATT_GROUPS * ATT_HEADS_PER_GROUP
    slopes = tuple(float(np.float32(2.0 ** (-8.0 * (g * ATT_HEADS_PER_GROUP + h + 1) / n_total)))
                   for h in range(ATT_HEADS_PER_GROUP))

    def cur(part):
        return pl.BlockSpec((None, ATT_QROWS, width), lambda r, m: (r, m, part))

    def prev(part):
        return pl.BlockSpec((None, ATT_RAD, width), lambda r, m: (r, jnp.maximum(m * halo_per_blk - 1, 0), part))

    def nxt(part):
        return pl.BlockSpec((None, ATT_RAD, width),
                            lambda r, m: (r, jnp.minimum((m + 1) * halo_per_blk, n_halo - 1), part))

    return pl.pallas_call(
        functools.partial(_attn_body, dilation=dilation, n_res=n_res, slopes=slopes),
        grid=(dilation, nblk),
        in_specs=[cur(0), prev(1), cur(1), nxt(1), prev(2), cur(2), nxt(2)],
        out_specs=[pl.BlockSpec((None, ATT_QROWS, width), lambda r, m: (r, m, 0)),
                   pl.BlockSpec((None, ATT_QROWS, LANES), lambda r, m: (r, m, 0))],
        out_shape=[jax.ShapeDtypeStruct((dilation, n_res, width), BF16),
                   jax.ShapeDtypeStruct((dilation, n_res, LANES), F32)],
        compiler_params=_cparams(("parallel", "parallel")),
    )(view, view, view, view, view, view, view)


def _to_token_order(o_ref, l_ref, dilation):
    if dilation == 1:
        return o_ref[...].astype(F32), l_ref[...]
    o = jnp.concatenate([o_ref[r] for r in range(dilation)], axis=0)
    l = jnp.concatenate([l_ref[r] for r in range(dilation)], axis=0)
    pt = _residue_permutation(TOKEN_ROWS, dilation, transpose=True)
    o_tok = jnp.dot(pt, o, preferred_element_type=F32)
    l_tok = None
    rem = l
    for _ in range(3):
        piece = rem.astype(BF16)
        term = jnp.dot(pt, piece, preferred_element_type=F32)
        l_tok = term if l_tok is None else l_tok + term
        rem = rem - piece.astype(F32)
    return o_tok, l_tok


def _attn_merge_body(o0_ref, o1_ref, o2_ref, l0_ref, l1_ref, l2_ref, o_ref):
    pairs = ((o0_ref, l0_ref), (o1_ref, l1_ref), (o2_ref, l2_ref))
    (o0, l0), (o1, l1), (o2, l2) = [_to_token_order(o, l, DILATIONS[g]) for g, (o, l) in enumerate(pairs)]
    mx = jnp.maximum(jnp.maximum(l0, l1), l2)
    e0, e1, e2 = jnp.exp(l0 - mx), jnp.exp(l1 - mx), jnp.exp(l2 - mx)
    inv = 1.0 / (e0 + e1 + e2)
    w0, w1, w2 = e0 * inv, e1 * inv, e2 * inv
    for h in range(ATT_HEADS_PER_GROUP):
        hs = slice(h * ATT_HEAD_DIM, (h + 1) * ATT_HEAD_DIM)
        acc = w0[:, h:h + 1] * o0[:, hs] + w1[:, h:h + 1] * o1[:, hs] + w2[:, h:h + 1] * o2[:, hs]
        o_ref[:, hs] = acc.astype(o_ref.dtype)


def attention_merge(outs, lses):
    t = outs[0].shape[0] * outs[0].shape[1]
    width = outs[0].shape[-1]

    def blk(g, last):
        d = DILATIONS[g]
        if d == 1:
            return pl.BlockSpec((None, TOKEN_ROWS, last), lambda i: (0, i, 0))
        return pl.BlockSpec((d, TOKEN_ROWS // d, last), lambda i: (0, i, 0))

    return pl.pallas_call(
        _attn_merge_body,
        grid=(t // TOKEN_ROWS,),
        in_specs=[blk(g, width) for g in range(N_ATT_GROUPS)] + [blk(g, LANES) for g in range(N_ATT_GROUPS)],
        out_specs=pl.BlockSpec((TOKEN_ROWS, width), lambda i: (i, 0)),
        out_shape=jax.ShapeDtypeStruct((t, width), BF16),
        compiler_params=_cparams(("parallel",)),
    )(*outs, *lses)


CONV_ROWS = 256
CONV_HALO = 16


def _conv_body(vp_ref, vc_ref, vn_ref, gp_ref, gc_ref, gn_ref, cw_ref, cb_ref, lg_ref, lb_ref, o_ref, u_ref):
    m = pl.program_id(0)
    last = pl.num_programs(0) - 1

    def glu(v_ref, g_ref):
        return v_ref[...].astype(F32) * jax.nn.sigmoid(g_ref[...].astype(F32))

    u_ref[0:CONV_HALO, :] = jnp.where(m > 0, glu(vp_ref, gp_ref), 0.0)
    u_ref[CONV_HALO:CONV_HALO + CONV_ROWS, :] = glu(vc_ref, gc_ref)
    u_ref[CONV_HALO + CONV_ROWS:, :] = jnp.where(m < last, glu(vn_ref, gn_ref), 0.0)
    base = CONV_HALO - CONV_WIDTH // 2
    acc = None
    for w in range(CONV_WIDTH):
        term = u_ref[base + w:base + w + CONV_ROWS, :] * cw_ref[w:w + 1, :]
        acc = term if acc is None else acc + term
    acc = acc + cb_ref[...]
    mu = jnp.mean(acc, axis=-1, keepdims=True)
    xc = acc - mu
    var = jnp.mean(jnp.square(xc), axis=-1, keepdims=True)
    y = xc * lax.rsqrt(var + LN_EPS) * lg_ref[...] + lb_ref[...]
    o_ref[...] = (y * jax.nn.sigmoid(y)).astype(o_ref.dtype)


def conv_module(h_conv, conv_w, conv_b, ln_g, ln_b):
    t = h_conv.shape[0]
    nb = t // CONV_ROWS
    halo_per_blk = CONV_ROWS // CONV_HALO
    n_halo = t // CONV_HALO

    def cur(c):
        return pl.BlockSpec((CONV_ROWS, CONV_CH), lambda m: (m, c))

    def prev(c):
        return pl.BlockSpec((CONV_HALO, CONV_CH), lambda m: (jnp.maximum(m * halo_per_blk - 1, 0), c))

    def nxt(c):
        return pl.BlockSpec((CONV_HALO, CONV_CH), lambda m: (jnp.minimum((m + 1) * halo_per_blk, n_halo - 1), c))

    vec = pl.BlockSpec((1, CONV_CH), lambda m: (0, 0))
    return pl.pallas_call(
        _conv_body,
        grid=(nb,),
        in_specs=[prev(0), cur(0), nxt(0), prev(1), cur(1), nxt(1),
                  pl.BlockSpec((CONV_WIDTH, CONV_CH), lambda m: (0, 0)), vec, vec, vec],
        out_specs=pl.BlockSpec((CONV_ROWS, CONV_CH), lambda m: (m, 0)),
        out_shape=jax.ShapeDtypeStruct((t, CONV_CH), BF16),
        scratch_shapes=[pltpu.VMEM((CONV_ROWS + 2 * CONV_HALO, CONV_CH), F32)],
        compiler_params=_cparams(("parallel",)),
    )(h_conv, h_conv, h_conv, h_conv, h_conv, h_conv, conv_w, conv_b, ln_g, ln_b)


def _merge_body(x_ref, oa_ref, ob_ref, oc_ref, wg_ref, bg_ref, wb_ref, o_ref):
    x = x_ref[...]
    acc = None
    for i, o_i in enumerate((oa_ref, ob_ref, oc_ref)):
        g = jnp.dot(x, wg_ref[i], preferred_element_type=F32) + bg_ref[i]
        br = jnp.dot(o_i[...], wb_ref[i], preferred_element_type=F32)
        term = jax.nn.sigmoid(g) * br
        acc = term if acc is None else acc + term
    o_ref[...] = acc.astype(o_ref.dtype)


def gated_merge(x, o_a, o_b, o_c, w_gate3, b_gate3, w_branch, tm=512, tn=512):
    t, d = x.shape
    bw = o_a.shape[1]
    o_spec = pl.BlockSpec((tm, bw), lambda i, j: (i, 0))
    return pl.pallas_call(
        _merge_body,
        grid=(t // tm, d // tn),
        in_specs=[pl.BlockSpec((tm, d), lambda i, j: (i, 0)), o_spec, o_spec, o_spec,
                  pl.BlockSpec((N_BRANCH, d, tn), lambda i, j: (0, 0, j)),
                  pl.BlockSpec((N_BRANCH, 1, tn), lambda i, j: (0, 0, j)),
                  pl.BlockSpec((N_BRANCH, bw, tn), lambda i, j: (0, 0, j))],
        out_specs=pl.BlockSpec((tm, tn), lambda i, j: (i, j)),
        out_shape=jax.ShapeDtypeStruct((t, d), BF16),
        compiler_params=_cparams(("parallel", "parallel")),
    )(x, o_a, o_b, o_c, w_gate3, b_gate3, w_branch)


def _add_ln(r_ref, y_ref, g_ref, b_ref):
    s = ALPHA * r_ref[...] + y_ref[...]
    mu = jnp.mean(s, axis=-1, keepdims=True)
    xc = s - mu
    var = jnp.mean(jnp.square(xc), axis=-1, keepdims=True)
    return xc * lax.rsqrt(var + LN_EPS) * g_ref[...] + b_ref[...]


def _ln_router_body(r_ref, y_ref, g_ref, b_ref, wr_ref, x_ref, lg_ref):
    xn = _add_ln(r_ref, y_ref, g_ref, b_ref)
    x_ref[...] = xn
    lg_ref[...] = jnp.dot(xn.astype(BF16), wr_ref[...], preferred_element_type=F32)


def _ln_views_body(r_ref, y_ref, g_ref, b_ref, x_ref, *xb_refs):
    xn = _add_ln(r_ref, y_ref, g_ref, b_ref)
    x_ref[...] = xn
    if xb_refs:
        _store_bf16_views(xn.astype(BF16), xb_refs)


def add_ln_router(res, y, g, b, w_router):
    t, d = res.shape
    row = pl.BlockSpec((TOKEN_ROWS, d), lambda i: (i, 0))
    vec = pl.BlockSpec((1, d), lambda i: (0, 0))
    return pl.pallas_call(
        _ln_router_body,
        grid=(t // TOKEN_ROWS,),
        in_specs=[row, row, vec, vec, pl.BlockSpec((d, LANES), lambda i: (0, 0))],
        out_specs=[row, pl.BlockSpec((TOKEN_ROWS, LANES), lambda i: (i, 0))],
        out_shape=[jax.ShapeDtypeStruct((t, d), F32), jax.ShapeDtypeStruct((t, LANES), F32)],
        compiler_params=_cparams(("parallel",)),
    )(res, y, g, b, w_router)


def add_ln_views(res, y, g, b, with_views):
    t, d = res.shape
    row = pl.BlockSpec((TOKEN_ROWS, d), lambda i: (i, 0))
    vec = pl.BlockSpec((1, d), lambda i: (0, 0))
    specs, shapes = _bf16_view_specs(t, d) if with_views else ([], [])
    return pl.pallas_call(
        _ln_views_body,
        grid=(t // TOKEN_ROWS,),
        in_specs=[row, row, vec, vec],
        out_specs=[row] + specs,
        out_shape=[jax.ShapeDtypeStruct((t, d), F32)] + shapes,
        compiler_params=_cparams(("parallel",)),
    )(res, y, g, b)


MOE_ROWS = 512
MOE_FF = 256


def _moe_body(idx_ref, idx_next_ref, gate_ref, w1_ref, w3_ref, w2_ref, x_hbm, m_in_hbm, m_hbm,
              xbuf, xe, acc, sems):
    del m_in_hbm
    e, i, f = pl.program_id(0), pl.program_id(1), pl.program_id(2)
    n_i, n_f = pl.num_programs(1), pl.num_programs(2)
    b = e * n_i + i
    n_blocks = pl.num_programs(0) * n_i
    rows = xbuf.shape[0]
    x_sem, m_sem, s_sem = sems.at[0], sems.at[1], sems.at[2]

    def row_copies(idx, issue):
        def body(j, carry):
            issue(idx[0, 0, j], j)
            return carry
        lax.fori_loop(0, rows, body, 0)

    def gather(src_hbm, idx, dst, sem):
        row_copies(idx, lambda row, j: pltpu.make_async_copy(
            src_hbm.at[pl.ds(row, 1)], dst.at[pl.ds(j, 1)], sem).start())

    def wait_gather(dst, sem):
        pltpu.make_async_copy(x_hbm.at[pl.ds(0, rows)], dst, sem).wait()

    def wait_scatter():
        pltpu.make_async_copy(acc, m_hbm.at[pl.ds(0, rows)], s_sem).wait()

    @pl.when(f == 0)
    def _():
        @pl.when(b == 0)
        def _():
            gather(x_hbm, idx_ref, xbuf, x_sem)

        @pl.when(b > 0)
        def _():
            wait_scatter()

        gather(m_hbm, idx_ref, acc, m_sem)
        wait_gather(xbuf, x_sem)
        xe[...] = xbuf[...].astype(BF16)

        @pl.when(b + 1 < n_blocks)
        def _():
            gather(x_hbm, idx_next_ref, xbuf, x_sem)

    x = xe[...]
    h1 = jnp.dot(x, w1_ref[0], preferred_element_type=F32)
    h3 = jnp.dot(x, w3_ref[0], preferred_element_type=F32)
    hid = (h1 * jax.nn.sigmoid(h1) * h3).astype(BF16)
    part = jnp.dot(hid, w2_ref[0], preferred_element_type=F32) * gate_ref[0]

    @pl.when(f == 0)
    def _():
        wait_gather(acc, m_sem)

    acc[...] += part

    @pl.when(f == n_f - 1)
    def _():
        row_copies(idx_ref, lambda row, j: pltpu.make_async_copy(
            acc.at[pl.ds(j, 1)], m_hbm.at[pl.ds(row, 1)], s_sem).start())

        @pl.when(b == n_blocks - 1)
        def _():
            wait_scatter()


def moe_scatter_add(x, idx, gate, w1, w3, w2):
    t, d = x.shape
    e, c = idx.shape
    ff = w1.shape[2]
    rows = min(MOE_ROWS, c)
    n_i = c // rows
    n_blocks = e * n_i
    idx_blocks = idx.reshape(n_blocks, 1, rows)
    smem = lambda index_map: pl.BlockSpec((1, 1, rows), index_map, memory_space=pltpu.SMEM)
    return pl.pallas_call(
        _moe_body,
        grid=(e, n_i, ff // MOE_FF),
        in_specs=[smem(lambda a, i, f: (a * n_i + i, 0, 0)),
                  smem(lambda a, i, f: (jnp.minimum(a * n_i + i + 1, n_blocks - 1), 0, 0)),
                  pl.BlockSpec((1, rows, 1), lambda a, i, f: (a, i, 0)),
                  pl.BlockSpec((1, d, MOE_FF), lambda a, i, f: (a, 0, f)),
                  pl.BlockSpec((1, d, MOE_FF), lambda a, i, f: (a, 0, f)),
                  pl.BlockSpec((1, MOE_FF, d), lambda a, i, f: (a, f, 0)),
                  pl.BlockSpec(memory_space=pl.ANY),
                  pl.BlockSpec(memory_space=pl.ANY)],
        out_specs=pl.BlockSpec(memory_space=pl.ANY),
        out_shape=jax.ShapeDtypeStruct((t, d), F32),
        input_output_aliases={7: 0},
        scratch_shapes=[pltpu.VMEM((rows, d), F32), pltpu.VMEM((rows, d), BF16), pltpu.VMEM((rows, d), F32),
                        pltpu.SemaphoreType.DMA((3,))],
        compiler_params=_cparams(("arbitrary", "arbitrary", "arbitrary")),
    )(idx_blocks, idx_blocks, gate[..., None], w1, w3, w2, x, jnp.zeros((t, d), F32))


def mixer_block(xb_views, lw):
    xb = xb_views[0]
    t = xb.shape[0]
    h_gla = matmul(xb, lw['w_gla'], BF16)
    h_ra = matmul(xb, lw['w_ra'], BF16, tn=LANES)
    h_conv = matmul(xb, lw['w_conv'], BF16)

    o_bwd = gla_pass(h_gla, h_ra, lw['wz'][1], lw['bz'][1])
    o_a = gla_pass(h_gla, h_ra, lw['wz'][0], lw['bz'][0], lw['gla_norm_g'], o_bwd)

    outs, lses = [], []
    for g in range(N_ATT_GROUPS):
        h_g = matmul(xb_views[g].reshape(t, D_MODEL), lw['w_att'][g], BF16)
        o_g, l_g = dilated_attention_group(h_g, g)
        outs.append(o_g)
        lses.append(l_g)
    o_b = attention_merge(outs, lses)

    o_c = conv_module(h_conv, lw['conv_w'], lw['conv_b'], lw['conv_ln_g'], lw['conv_ln_b'])

    merged = gated_merge(xb, o_a, o_b, o_c, lw['w_gate3'], lw['b_gate3'], lw['w_branch'])
    return matmul(merged, lw['w_out'], F32)


def expert_choice_moe(x, logits, lw):
    n_tok = x.shape[0]
    cap = max(1, EC_FACTOR * n_tok // N_EXPERTS)
    probs = jax.nn.softmax(logits[:, :N_EXPERTS], axis=-1)
    gate, idx = lax.top_k(probs.T, cap)
    return moe_scatter_add(x, idx, gate, lw['w_e1'], lw['w_e3'], lw['w_e2'])


def encoder_trunk(x, layers):
    x = x[0]
    xb_views = cast_views(x)
    for li, lw in enumerate(layers):
        y = mixer_block(xb_views, lw)
        x, logits = add_ln_router(x, y, lw['ln1_g'], lw['ln1_b'], lw['w_router'])
        m = expert_choice_moe(x, logits, lw)
        x, *xb_views = add_ln_views(x, m, lw['ln2_g'], lw['ln2_b'], with_views=li + 1 < len(layers))
    return x[None]


def _prep_layer(l, w_in, w_gla_gate, b_gla_gate, gla_norm_g, conv_w, conv_b, conv_ln_g, conv_ln_b,
                w_branch, w_gate, b_gate, w_out, ln1_g, ln1_b, w_router, w_e1, w_e3, w_e2, ln2_g, ln2_b):
    o_ra, o_att, o_conv = IN_OFFSETS[3], IN_OFFSETS[4], IN_OFFSETS[7]
    wi = w_in[l]
    wz = jnp.zeros((2, LANES, GLA_QK), F32)
    wz = wz.at[0, :GLA_RANK].set(w_gla_gate[l, 0]).at[1, GLA_RANK:2 * GLA_RANK].set(w_gla_gate[l, 1])
    w_att = wi[:, o_att:o_conv].reshape(D_MODEL, 3, N_ATT_GROUPS, ATT_OUT).transpose(2, 0, 1, 3)
    row = lambda a: a.reshape(1, -1)
    return {
        'w_gla': wi[:, :o_ra].astype(BF16),
        'w_ra': jnp.pad(wi[:, o_ra:o_att], ((0, 0), (0, LANES - 2 * GLA_RANK))).astype(BF16),
        'w_att': w_att.reshape(N_ATT_GROUPS, D_MODEL, 3 * ATT_OUT).astype(BF16),
        'w_conv': wi[:, o_conv:].astype(BF16),
        'wz': wz.astype(BF16), 'bz': b_gla_gate[l].reshape(2, 1, GLA_QK),
        'gla_norm_g': row(gla_norm_g[l]),
        'conv_w': conv_w[l], 'conv_b': row(conv_b[l]), 'conv_ln_g': row(conv_ln_g[l]), 'conv_ln_b': row(conv_ln_b[l]),
        'w_branch': w_branch[l].astype(BF16),
        'w_gate3': w_gate[l].reshape(D_MODEL, N_BRANCH, D_MODEL).transpose(1, 0, 2).astype(BF16),
        'b_gate3': b_gate[l].reshape(N_BRANCH, 1, D_MODEL),
        'w_out': w_out[l].astype(BF16),
        'ln1_g': row(ln1_g[l]), 'ln1_b': row(ln1_b[l]),
        'w_router': jnp.pad(w_router[l], ((0, 0), (0, LANES - N_EXPERTS))).astype(BF16),
        'w_e1': w_e1[l].astype(BF16), 'w_e3': w_e3[l].astype(BF16), 'w_e2': w_e2[l].astype(BF16),
        'ln2_g': row(ln2_g[l]), 'ln2_b': row(ln2_b[l]),
    }


def kernel(x_prompt, x_sample, w_in, w_gla_gate, b_gla_gate, gla_norm_g, conv_w, conv_b, conv_ln_g, conv_ln_b, w_branch, w_gate, b_gate, w_out, ln1_g, ln1_b, w_router, w_e1, w_e3, w_e2, ln2_g, ln2_b):
    params = (w_in, w_gla_gate, b_gla_gate, gla_norm_g, conv_w, conv_b, conv_ln_g, conv_ln_b,
              w_branch, w_gate, b_gate, w_out, ln1_g, ln1_b, w_router, w_e1, w_e3, w_e2, ln2_g, ln2_b)
    layers = [_prep_layer(l, *params) for l in range(DEPTH)]
    return (encoder_trunk(x_prompt, layers), encoder_trunk(x_sample, layers))
```

```python
import functools

import jax
import jax.numpy as jnp
import numpy as np
from jax import lax
from jax.experimental import pallas as pl
from jax.experimental.pallas import tpu as pltpu

D_MODEL = 4096
DEPTH = 4
GLA_HEADS = 4
GLA_DK = 128
GLA_DV = 256
GLA_RANK = 16
GLA_TEMP = 16.0
GLA_CHUNK = 64
ATT_GROUPS = ((128, 1), (512, 4), (2048, 16))
ATT_HEADS_PER_GROUP = 8
ATT_HEAD_DIM = 128
CONV_CH = 1024
CONV_WIDTH = 31
N_BRANCH = 3
BRANCH_WIDTH = 1024
N_EXPERTS = 16
EXPERT_FF = 2048
EC_FACTOR = 2
LN_EPS = 1e-5
NEG_INF = -1e30

GLA_QK = GLA_HEADS * GLA_DK
GLA_V = GLA_HEADS * GLA_DV
N_ATT_GROUPS = len(ATT_GROUPS)
ATT_QKV = N_ATT_GROUPS * ATT_HEADS_PER_GROUP * ATT_HEAD_DIM
ATT_OUT = ATT_HEADS_PER_GROUP * ATT_HEAD_DIM
IN_WIDTHS = (GLA_QK, GLA_QK, GLA_V, GLA_V, 2 * GLA_RANK, ATT_QKV, ATT_QKV, ATT_QKV, 2 * CONV_CH)
IN_OFFSETS = tuple(int(v) for v in np.cumsum(IN_WIDTHS)[:-1])
ALPHA = (2.0 * DEPTH) ** 0.25

V7X_VMEM_LIMIT_BYTES = 56 * 1024 * 1024
LANES = 128
F32_SUBLANES = 8

BF16 = jnp.bfloat16
F32 = jnp.float32

_NT = (((1,), (1,)), ((), ()))
_TN = (((0,), (0,)), ((), ()))


def _cparams(sem):
    return pltpu.CompilerParams(dimension_semantics=sem, vmem_limit_bytes=V7X_VMEM_LIMIT_BYTES)


def _mm_body(x_ref, w_ref, o_ref):
    o_ref[...] = jnp.dot(x_ref[...], w_ref[...], preferred_element_type=F32).astype(o_ref.dtype)


def matmul(x, w, out_dtype, tm=1024, tn=1024):
    m, k = x.shape
    n = w.shape[1]
    tm, tn = min(tm, m), min(tn, n)
    assert m % tm == 0 and n % tn == 0
    return pl.pallas_call(
        _mm_body,
        grid=(m // tm, n // tn),
        in_specs=[pl.BlockSpec((tm, k), lambda i, j: (i, 0)),
                  pl.BlockSpec((k, tn), lambda i, j: (0, j))],
        out_specs=pl.BlockSpec((tm, tn), lambda i, j: (i, j)),
        out_shape=jax.ShapeDtypeStruct((m, n), out_dtype),
        compiler_params=_cparams(("parallel", "parallel")),
    )(x, w)


TOKEN_ROWS = 256
DILATIONS = tuple(d for _, d in ATT_GROUPS)


def _residue_permutation(n, dilation, transpose):
    per = n // dilation
    shift_per, shift_d = per.bit_length() - 1, dilation.bit_length() - 1
    i = lax.broadcasted_iota(jnp.int32, (n, n), 1 if transpose else 0)
    j = lax.broadcasted_iota(jnp.int32, (n, n), 0 if transpose else 1)
    tok = ((i & (per - 1)) << shift_d) | (i >> shift_per)
    return (tok == j).astype(BF16)


def _store_bf16_views(xb, xb_refs):
    for d, ref in zip(DILATIONS, xb_refs):
        if d == 1:
            ref[...] = xb
        else:
            perm = jnp.dot(_residue_permutation(TOKEN_ROWS, d, transpose=False), xb,
                           preferred_element_type=F32).astype(BF16)
            per = TOKEN_ROWS // d
            for r in range(d):
                ref[r] = perm[r * per:(r + 1) * per]


def _bf16_view_specs(t, d_model):
    specs, shapes = [], []
    for d in DILATIONS:
        if d == 1:
            specs.append(pl.BlockSpec((TOKEN_ROWS, d_model), lambda i: (i, 0)))
            shapes.append(jax.ShapeDtypeStruct((t, d_model), BF16))
        else:
            specs.append(pl.BlockSpec((d, TOKEN_ROWS // d, d_model), lambda i: (0, i, 0)))
            shapes.append(jax.ShapeDtypeStruct((d, t // d, d_model), BF16))
    return specs, shapes


def _cast_views_body(x_ref, *xb_refs):
    _store_bf16_views(x_ref[...].astype(BF16), xb_refs)


def cast_views(x):
    t, d_model = x.shape
    specs, shapes = _bf16_view_specs(t, d_model)
    return pl.pallas_call(
        _cast_views_body,
        grid=(t // TOKEN_ROWS,),
        in_specs=[pl.BlockSpec((TOKEN_ROWS, d_model), lambda i: (i, 0))],
        out_specs=specs,
        out_shape=shapes,
        compiler_params=_cparams(("parallel",)),
    )(x)


GLA_ROWS = 256


def _log_sigmoid(z):
    return jnp.minimum(z, 0.0) - jnp.log(1.0 + jnp.exp(-jnp.abs(z)))


def _gla_body(*refs, reverse, finalize):
    if finalize:
        q_ref, k_ref, v_ref, g_ref, ra_ref, wz_ref, bz_ref, ng_ref, ob_ref, o_ref, st_ref = refs
    else:
        q_ref, k_ref, v_ref, ra_ref, wz_ref, bz_ref, o_ref, st_ref = refs

    @pl.when(pl.program_id(0) == 0)
    def _():
        st_ref[...] = jnp.zeros_like(st_ref)

    z = jnp.dot(ra_ref[...], wz_ref[...], preferred_element_type=F32) + bz_ref[...]
    log_a = _log_sigmoid(z) / GLA_TEMP
    ri = lax.broadcasted_iota(jnp.int32, (GLA_CHUNK, GLA_CHUNK), 0)
    ci = lax.broadcasted_iota(jnp.int32, (GLA_CHUNK, GLA_CHUNK), 1)
    tri = (ci >= ri) if reverse else (ci <= ri)
    tri_b = tri.astype(BF16)
    n_chunk = q_ref.shape[0] // GLA_CHUNK
    order = range(n_chunk - 1, -1, -1) if reverse else range(n_chunk)
    for c in order:
        rows = slice(c * GLA_CHUNK, (c + 1) * GLA_CHUNK)
        la = log_a[rows]
        la_hi = la.astype(BF16)
        la_lo = (la - la_hi.astype(F32)).astype(BF16)
        bcum = (jnp.dot(tri_b, la_hi, preferred_element_type=F32)
                + jnp.dot(tri_b, la_lo, preferred_element_type=F32))
        b_last = bcum[0:1] if reverse else bcum[GLA_CHUNK - 1:GLA_CHUNK]
        e_pos = jnp.exp(bcum)
        e_neg = jnp.exp(-bcum)
        e_rem = jnp.exp(b_last - bcum)
        decay = jnp.exp(b_last)
        for h in range(GLA_HEADS):
            hk = slice(h * GLA_DK, (h + 1) * GLA_DK)
            hv = slice(h * GLA_DV, (h + 1) * GLA_DV)
            q = q_ref[rows, hk].astype(F32) * (GLA_DK ** -0.5)
            k = k_ref[rows, hk].astype(F32)
            v = v_ref[rows, hv]
            q_e = (q * e_pos[:, hk]).astype(BF16)
            k_e = (k * e_neg[:, hk]).astype(BF16)
            k_s = (k * e_rem[:, hk]).astype(BF16)
            attn = lax.dot_general(q_e, k_e, _NT, preferred_element_type=F32)
            attn = jnp.where(tri, attn, 0.0).astype(BF16)
            st = st_ref[h]
            o = (jnp.dot(attn, v, preferred_element_type=F32)
                 + lax.dot_general(q_e, st.astype(BF16), _NT, preferred_element_type=F32))
            d_state = lax.dot_general(v, k_s, _TN, preferred_element_type=F32)
            st_ref[h] = decay[:, hk] * st + d_state
            if finalize:
                o = o + ob_ref[rows, hv]
                o = o * lax.rsqrt(jnp.mean(jnp.square(o), axis=-1, keepdims=True) + LN_EPS) * ng_ref[...]
                g = g_ref[rows, hv].astype(F32)
                o = o * (g * jax.nn.sigmoid(g))
            o_ref[rows, hv] = o.astype(o_ref.dtype)


def gla_pass(h_gla, h_ra, wz, bz, norm_g=None, o_other=None):
    t = h_gla.shape[0]
    finalize = o_other is not None
    reverse = not finalize
    nb = t // GLA_ROWS
    blk = (lambda n: nb - 1 - n) if reverse else (lambda n: n)
    in_specs = [pl.BlockSpec((GLA_ROWS, GLA_QK), lambda n: (blk(n), 0)),
                pl.BlockSpec((GLA_ROWS, GLA_QK), lambda n: (blk(n), 1)),
                pl.BlockSpec((GLA_ROWS, GLA_V), lambda n: (blk(n), 1))]
    args = [h_gla, h_gla, h_gla]
    if finalize:
        in_specs.append(pl.BlockSpec((GLA_ROWS, GLA_V), lambda n: (blk(n), 2)))
        args.append(h_gla)
    in_specs += [pl.BlockSpec((GLA_ROWS, LANES), lambda n: (blk(n), 0)),
                 pl.BlockSpec((LANES, GLA_QK), lambda n: (0, 0)),
                 pl.BlockSpec((1, GLA_QK), lambda n: (0, 0))]
    args += [h_ra, wz, bz]
    if finalize:
        in_specs += [pl.BlockSpec((1, GLA_DV), lambda n: (0, 0)),
                     pl.BlockSpec((GLA_ROWS, GLA_V), lambda n: (blk(n), 0))]
        args += [norm_g, o_other]
    return pl.pallas_call(
        functools.partial(_gla_body, reverse=reverse, finalize=finalize),
        grid=(nb,),
        in_specs=in_specs,
        out_specs=pl.BlockSpec((GLA_ROWS, GLA_V), lambda n: (blk(n), 0)),
        out_shape=jax.ShapeDtypeStruct((t, GLA_V), BF16 if finalize else F32),
        scratch_shapes=[pltpu.VMEM((GLA_HEADS, GLA_DV, GLA_DK), F32)],
        compiler_params=_cparams(("arbitrary",)),
    )(*args)


ATT_RAD = 64
ATT_QROWS = 256
ATT_SUB = 128


def _attn_body(q_ref, kp_ref, kc_ref, kn_ref, vp_ref, vc_ref, vn_ref, o_ref, lse_ref, *, dilation, n_res, slopes):
    m = pl.program_id(1)
    n_sub = ATT_QROWS // ATT_SUB
    n_keys = ATT_SUB + 2 * ATT_RAD
    qi = lax.broadcasted_iota(jnp.int32, (ATT_SUB, n_keys), 0)
    kj = lax.broadcasted_iota(jnp.int32, (ATT_SUB, n_keys), 1)
    dist = jnp.abs(kj - ATT_RAD - qi)
    dist_f = (dilation * dist).astype(F32)
    valid = []
    for j in range(n_sub):
        key_row = m * ATT_QROWS + j * ATT_SUB - ATT_RAD + kj
        valid.append((dist <= ATT_RAD) & (key_row >= 0) & (key_row < n_res))
    lane = lax.broadcasted_iota(jnp.int32, (ATT_SUB, LANES), 1)
    lse_tiles = [jnp.zeros((ATT_SUB, LANES), F32) for _ in range(n_sub)]
    for h in range(ATT_HEADS_PER_GROUP):
        hs = slice(h * ATT_HEAD_DIM, (h + 1) * ATT_HEAD_DIM)
        k_all = jnp.concatenate([kp_ref[:, hs], kc_ref[:, hs], kn_ref[:, hs]], axis=0)
        v_all = jnp.concatenate([vp_ref[:, hs], vc_ref[:, hs], vn_ref[:, hs]], axis=0)
        for j in range(n_sub):
            q = q_ref[j * ATT_SUB:(j + 1) * ATT_SUB, hs]
            k = k_all[j * ATT_SUB:j * ATT_SUB + n_keys]
            v = v_all[j * ATT_SUB:j * ATT_SUB + n_keys]
            s = lax.dot_general(q, k, _NT, preferred_element_type=F32) * (ATT_HEAD_DIM ** -0.5)
            s = jnp.where(valid[j], s - slopes[h] * dist_f, NEG_INF)
            mx = jnp.max(s, axis=-1, keepdims=True)
            p = jnp.exp(s - mx)
            l = jnp.sum(p, axis=-1, keepdims=True)
            o = jnp.dot(p.astype(BF16), v, preferred_element_type=F32) / l
            o_ref[j * ATT_SUB:(j + 1) * ATT_SUB, hs] = o.astype(o_ref.dtype)
            lse_tiles[j] = jnp.where(lane == h, mx + jnp.log(l), lse_tiles[j])
    for j in range(n_sub):
        lse_ref[j * ATT_SUB:(j + 1) * ATT_SUB, :] = lse_tiles[j]


def dilated_attention_group(h_g, g):
    window, dilation = ATT_GROUPS[g]
    assert window // (2 * dilation) == ATT_RAD
    t = h_g.shape[0]
    n_res = t // dilation
    assert n_res % ATT_QROWS == 0
    nblk = n_res // ATT_QROWS
    halo_per_blk = ATT_QROWS // ATT_RAD
    n_halo = n_res // ATT_RAD
    width = ATT_HEADS_PER_GROUP * ATT_HEAD_DIM
    view = h_g.reshape(dilation, n_res, 3 * width)
    n_total = N_ATT_GROUPS * ATT_HEADS_PER_GROUP
    slopes = tuple(float(np.float32(2.0 ** (-8.0 * (g * ATT_HEADS_PER_GROUP + h + 1) / n_total)))
                   for h in range(ATT_HEADS_PER_GROUP))

    def cur(part):
        return pl.BlockSpec((None, ATT_QROWS, width), lambda r, m: (r, m, part))

    def prev(part):
        return pl.BlockSpec((None, ATT_RAD, width), lambda r, m: (r, jnp.maximum(m * halo_per_blk - 1, 0), part))

    def nxt(part):
        return pl.BlockSpec((None, ATT_RAD, width),
                            lambda r, m: (r, jnp.minimum((m + 1) * halo_per_blk, n_halo - 1), part))

    return pl.pallas_call(
        functools.partial(_attn_body, dilation=dilation, n_res=n_res, slopes=slopes),
        grid=(dilation, nblk),
        in_specs=[cur(0), prev(1), cur(1), nxt(1), prev(2), cur(2), nxt(2)],
        out_specs=[pl.BlockSpec((None, ATT_QROWS, width), lambda r, m: (r, m, 0)),
                   pl.BlockSpec((None, ATT_QROWS, LANES), lambda r, m: (r, m, 0))],
        out_shape=[jax.ShapeDtypeStruct((dilation, n_res, width), BF16),
                   jax.ShapeDtypeStruct((dilation, n_res, LANES), F32)],
        compiler_params=_cparams(("parallel", "parallel")),
    )(view, view, view, view, view, view, view)


def _to_token_order(o_ref, l_ref, dilation):
    if dilation == 1:
        return o_ref[...].astype(F32), l_ref[...]
    o = jnp.concatenate([o_ref[r] for r in range(dilation)], axis=0)
    l = jnp.concatenate([l_ref[r] for r in range(dilation)], axis=0)
    pt = _residue_permutation(TOKEN_ROWS, dilation, transpose=True)
    o_tok = jnp.dot(pt, o, preferred_element_type=F32)
    l_tok = None
    rem = l
    for _ in range(3):
        piece = rem.astype(BF16)
        term = jnp.dot(pt, piece, preferred_element_type=F32)
        l_tok = term if l_tok is None else l_tok + term
        rem = rem - piece.astype(F32)
    return o_tok, l_tok


def _attn_merge_body(o0_ref, o1_ref, o2_ref, l0_ref, l1_ref, l2_ref, o_ref):
    pairs = ((o0_ref, l0_ref), (o1_ref, l1_ref), (o2_ref, l2_ref))
    (o0, l0), (o1, l1), (o2, l2) = [_to_token_order(o, l, DILATIONS[g]) for g, (o, l) in enumerate(pairs)]
    mx = jnp.maximum(jnp.maximum(l0, l1), l2)
    e0, e1, e2 = jnp.exp(l0 - mx), jnp.exp(l1 - mx), jnp.exp(l2 - mx)
    inv = 1.0 / (e0 + e1 + e2)
    w0, w1, w2 = e0 * inv, e1 * inv, e2 * inv
    for h in range(ATT_HEADS_PER_GROUP):
        hs = slice(h * ATT_HEAD_DIM, (h + 1) * ATT_HEAD_DIM)
        acc = w0[:, h:h + 1] * o0[:, hs] + w1[:, h:h + 1] * o1[:, hs] + w2[:, h:h + 1] * o2[:, hs]
        o_ref[:, hs] = acc.astype(o_ref.dtype)


def attention_merge(outs, lses):
    t = outs[0].shape[0] * outs[0].shape[1]
    width = outs[0].shape[-1]

    def blk(g, last):
        d = DILATIONS[g]
        if d == 1:
            return pl.BlockSpec((None, TOKEN_ROWS, last), lambda i: (0, i, 0))
        return pl.BlockSpec((d, TOKEN_ROWS // d, last), lambda i: (0, i, 0))

    return pl.pallas_call(
        _attn_merge_body,
        grid=(t // TOKEN_ROWS,),
        in_specs=[blk(g, width) for g in range(N_ATT_GROUPS)] + [blk(g, LANES) for g in range(N_ATT_GROUPS)],
        out_specs=pl.BlockSpec((TOKEN_ROWS, width), lambda i: (i, 0)),
        out_shape=jax.ShapeDtypeStruct((t, width), BF16),
        compiler_params=_cparams(("parallel",)),
    )(*outs, *lses)


CONV_ROWS = 256
CONV_HALO = 16


CONV_PHASE_ROWS = CONV_ROWS + 2 * CONV_HALO - F32_SUBLANES


def _conv_body(vp_ref, vc_ref, vn_ref, gp_ref, gc_ref, gn_ref, cw_ref, cb_ref, lg_ref, lb_ref, o_ref, u_ref, ph_ref):
    m = pl.program_id(0)
    last = pl.num_programs(0) - 1

    def glu(v_ref, g_ref):
        return v_ref[...].astype(F32) * jax.nn.sigmoid(g_ref[...].astype(F32))

    u_ref[0:CONV_HALO, :] = jnp.where(m > 0, glu(vp_ref, gp_ref), 0.0)
    u_ref[CONV_HALO:CONV_HALO + CONV_ROWS, :] = glu(vc_ref, gc_ref)
    u_ref[CONV_HALO + CONV_ROWS:, :] = jnp.where(m < last, glu(vn_ref, gn_ref), 0.0)
    base = CONV_HALO - CONV_WIDTH // 2
    for b in range(F32_SUBLANES):
        ph_ref[b] = u_ref[b:b + CONV_PHASE_ROWS, :]
    acc = None
    for w in range(CONV_WIDTH):
        a, b = divmod(base + w, F32_SUBLANES)
        term = ph_ref[b, a * F32_SUBLANES:a * F32_SUBLANES + CONV_ROWS, :] * cw_ref[w:w + 1, :]
        acc = term if acc is None else acc + term
    acc = acc + cb_ref[...]
    mu = jnp.mean(acc, axis=-1, keepdims=True)
    xc = acc - mu
    var = jnp.mean(jnp.square(xc), axis=-1, keepdims=True)
    y = xc * lax.rsqrt(var + LN_EPS) * lg_ref[...] + lb_ref[...]
    o_ref[...] = (y * jax.nn.sigmoid(y)).astype(o_ref.dtype)


def conv_module(h_conv, conv_w, conv_b, ln_g, ln_b):
    t = h_conv.shape[0]
    nb = t // CONV_ROWS
    halo_per_blk = CONV_ROWS // CONV_HALO
    n_halo = t // CONV_HALO

    def cur(c):
        return pl.BlockSpec((CONV_ROWS, CONV_CH), lambda m: (m, c))

    def prev(c):
        return pl.BlockSpec((CONV_HALO, CONV_CH), lambda m: (jnp.maximum(m * halo_per_blk - 1, 0), c))

    def nxt(c):
        return pl.BlockSpec((CONV_HALO, CONV_CH), lambda m: (jnp.minimum((m + 1) * halo_per_blk, n_halo - 1), c))

    vec = pl.BlockSpec((1, CONV_CH), lambda m: (0, 0))
    return pl.pallas_call(
        _conv_body,
        grid=(nb,),
        in_specs=[prev(0), cur(0), nxt(0), prev(1), cur(1), nxt(1),
                  pl.BlockSpec((CONV_WIDTH, CONV_CH), lambda m: (0, 0)), vec, vec, vec],
        out_specs=pl.BlockSpec((CONV_ROWS, CONV_CH), lambda m: (m, 0)),
        out_shape=jax.ShapeDtypeStruct((t, CONV_CH), BF16),
        scratch_shapes=[pltpu.VMEM((CONV_ROWS + 2 * CONV_HALO, CONV_CH), F32),
                        pltpu.VMEM((F32_SUBLANES, CONV_PHASE_ROWS, CONV_CH), F32)],
        compiler_params=_cparams(("parallel",)),
    )(h_conv, h_conv, h_conv, h_conv, h_conv, h_conv, conv_w, conv_b, ln_g, ln_b)


def _merge_body(x_ref, oa_ref, ob_ref, oc_ref, wg_ref, bg_ref, wb_ref, o_ref):
    x = x_ref[...]
    acc = None
    for i, o_i in enumerate((oa_ref, ob_ref, oc_ref)):
        g = jnp.dot(x, wg_ref[i], preferred_element_type=F32) + bg_ref[i]
        br = jnp.dot(o_i[...], wb_ref[i], preferred_element_type=F32)
        term = jax.nn.sigmoid(g) * br
        acc = term if acc is None else acc + term
    o_ref[...] = acc.astype(o_ref.dtype)


def gated_merge(x, o_a, o_b, o_c, w_gate3, b_gate3, w_branch, tm=512, tn=512):
    t, d = x.shape
    bw = o_a.shape[1]
    o_spec = pl.BlockSpec((tm, bw), lambda i, j: (i, 0))
    return pl.pallas_call(
        _merge_body,
        grid=(t // tm, d // tn),
        in_specs=[pl.BlockSpec((tm, d), lambda i, j: (i, 0)), o_spec, o_spec, o_spec,
                  pl.BlockSpec((N_BRANCH, d, tn), lambda i, j: (0, 0, j)),
                  pl.BlockSpec((N_BRANCH, 1, tn), lambda i, j: (0, 0, j)),
                  pl.BlockSpec((N_BRANCH, bw, tn), lambda i, j: (0, 0, j))],
        out_specs=pl.BlockSpec((tm, tn), lambda i, j: (i, j)),
        out_shape=jax.ShapeDtypeStruct((t, d), BF16),
        compiler_params=_cparams(("parallel", "parallel")),
    )(x, o_a, o_b, o_c, w_gate3, b_gate3, w_branch)


def _add_ln(r_ref, y_ref, g_ref, b_ref):
    s = ALPHA * r_ref[...] + y_ref[...].astype(F32)
    mu = jnp.mean(s, axis=-1, keepdims=True)
    xc = s - mu
    var = jnp.mean(jnp.square(xc), axis=-1, keepdims=True)
    return xc * lax.rsqrt(var + LN_EPS) * g_ref[...] + b_ref[...]


def _ln_router_body(r_ref, y_ref, g_ref, b_ref, wr_ref, x_ref, lg_ref):
    xn = _add_ln(r_ref, y_ref, g_ref, b_ref)
    x_ref[...] = xn
    lg_ref[...] = jnp.dot(xn.astype(BF16), wr_ref[...], preferred_element_type=F32)


def _ln_views_body(r_ref, y_ref, g_ref, b_ref, x_ref, *xb_refs):
    xn = _add_ln(r_ref, y_ref, g_ref, b_ref)
    x_ref[...] = xn
    if xb_refs:
        _store_bf16_views(xn.astype(BF16), xb_refs)


def add_ln_router(res, y, g, b, w_router):
    t, d = res.shape
    row = pl.BlockSpec((TOKEN_ROWS, d), lambda i: (i, 0))
    vec = pl.BlockSpec((1, d), lambda i: (0, 0))
    return pl.pallas_call(
        _ln_router_body,
        grid=(t // TOKEN_ROWS,),
        in_specs=[row, row, vec, vec, pl.BlockSpec((d, LANES), lambda i: (0, 0))],
        out_specs=[row, pl.BlockSpec((TOKEN_ROWS, LANES), lambda i: (i, 0))],
        out_shape=[jax.ShapeDtypeStruct((t, d), F32), jax.ShapeDtypeStruct((t, LANES), F32)],
        compiler_params=_cparams(("parallel",)),
    )(res, y, g, b, w_router)


def add_ln_views(res, y, g, b, with_views):
    t, d = res.shape
    row = pl.BlockSpec((TOKEN_ROWS, d), lambda i: (i, 0))
    vec = pl.BlockSpec((1, d), lambda i: (0, 0))
    specs, shapes = _bf16_view_specs(t, d) if with_views else ([], [])
    return pl.pallas_call(
        _ln_views_body,
        grid=(t // TOKEN_ROWS,),
        in_specs=[row, row, vec, vec],
        out_specs=[row] + specs,
        out_shape=[jax.ShapeDtypeStruct((t, d), F32)] + shapes,
        compiler_params=_cparams(("parallel",)),
    )(res, y, g, b)


MOE_ROWS = 512
MOE_FF = 256


def _moe_body(idx_ref, idx_next_ref, gate_ref, w1_ref, w3_ref, w2_ref, x_hbm, m_in_hbm, m_hbm,
              xbuf, xe, acc, sems):
    del m_in_hbm
    e, i, f = pl.program_id(0), pl.program_id(1), pl.program_id(2)
    n_i, n_f = pl.num_programs(1), pl.num_programs(2)
    b = e * n_i + i
    n_blocks = pl.num_programs(0) * n_i
    n_tiles, tile_rows, d = xbuf.shape
    rows = n_tiles * tile_rows
    x_sem, m_sem, s_sem = sems.at[0], sems.at[1], sems.at[2]

    def row_copies(idx, issue):
        def body(g, carry):
            for k in range(tile_rows):
                issue(idx[0, 0, g * tile_rows + k], g, k)
            return carry
        lax.fori_loop(0, n_tiles, body, 0)

    def gather(src_hbm, idx, dst, sem):
        row_copies(idx, lambda row, g, k: pltpu.make_async_copy(
            src_hbm.at[pl.ds(row, 1)], dst.at[g, pl.ds(k, 1)], sem).start())

    def wait_gather(dst, sem):
        pltpu.make_async_copy(dst, dst, sem).wait()

    def wait_scatter():
        pltpu.make_async_copy(acc, acc, s_sem).wait()

    @pl.when(f == 0)
    def _():
        @pl.when(b == 0)
        def _():
            gather(x_hbm, idx_ref, xbuf, x_sem)

        @pl.when(b > 0)
        def _():
            wait_scatter()

        gather(m_hbm, idx_ref, acc, m_sem)
        wait_gather(xbuf, x_sem)
        xe[...] = xbuf[...].reshape(rows, d).astype(BF16)

        @pl.when(b + 1 < n_blocks)
        def _():
            gather(x_hbm, idx_next_ref, xbuf, x_sem)

    x = xe[...]
    h1 = jnp.dot(x, w1_ref[0], preferred_element_type=F32)
    h3 = jnp.dot(x, w3_ref[0], preferred_element_type=F32)
    hid = (h1 * jax.nn.sigmoid(h1) * h3).astype(BF16)
    part = jnp.dot(hid, w2_ref[0], preferred_element_type=F32) * gate_ref[0]

    @pl.when(f == 0)
    def _():
        wait_gather(acc, m_sem)

    acc[...] += part.reshape(n_tiles, tile_rows, d)

    @pl.when(f == n_f - 1)
    def _():
        row_copies(idx_ref, lambda row, g, k: pltpu.make_async_copy(
            acc.at[g, pl.ds(k, 1)], m_hbm.at[pl.ds(row, 1)], s_sem).start())

        @pl.when(b == n_blocks - 1)
        def _():
            wait_scatter()


def moe_scatter_add(x, idx, gate, w1, w3, w2):
    t, d = x.shape
    e, c = idx.shape
    ff = w1.shape[2]
    rows = min(MOE_ROWS, c)
    n_i = c // rows
    n_blocks = e * n_i
    idx_blocks = idx.reshape(n_blocks, 1, rows)
    smem = lambda index_map: pl.BlockSpec((1, 1, rows), index_map, memory_space=pltpu.SMEM)
    return pl.pallas_call(
        _moe_body,
        grid=(e, n_i, ff // MOE_FF),
        in_specs=[smem(lambda a, i, f: (a * n_i + i, 0, 0)),
                  smem(lambda a, i, f: (jnp.minimum(a * n_i + i + 1, n_blocks - 1), 0, 0)),
                  pl.BlockSpec((1, rows, 1), lambda a, i, f: (a, i, 0)),
                  pl.BlockSpec((1, d, MOE_FF), lambda a, i, f: (a, 0, f)),
                  pl.BlockSpec((1, d, MOE_FF), lambda a, i, f: (a, 0, f)),
                  pl.BlockSpec((1, MOE_FF, d), lambda a, i, f: (a, f, 0)),
                  pl.BlockSpec(memory_space=pl.ANY),
                  pl.BlockSpec(memory_space=pl.ANY)],
        out_specs=pl.BlockSpec(memory_space=pl.ANY),
        out_shape=jax.ShapeDtypeStruct((t, d), F32),
        input_output_aliases={7: 0},
        scratch_shapes=[pltpu.VMEM((rows // F32_SUBLANES, F32_SUBLANES, d), F32), pltpu.VMEM((rows, d), BF16),
                        pltpu.VMEM((rows // F32_SUBLANES, F32_SUBLANES, d), F32),
                        pltpu.SemaphoreType.DMA((3,))],
        compiler_params=_cparams(("arbitrary", "arbitrary", "arbitrary")),
    )(idx_blocks, idx_blocks, gate[..., None], w1, w3, w2, x, jnp.zeros((t, d), F32))


def mixer_block(xb_views, lw):
    xb = xb_views[0]
    t = xb.shape[0]
    h_gla = matmul(xb, lw['w_gla'], BF16)
    h_ra = matmul(xb, lw['w_ra'], BF16, tn=LANES)
    h_conv = matmul(xb, lw['w_conv'], BF16)

    o_bwd = gla_pass(h_gla, h_ra, lw['wz'][1], lw['bz'][1])
    o_a = gla_pass(h_gla, h_ra, lw['wz'][0], lw['bz'][0], lw['gla_norm_g'], o_bwd)

    outs, lses = [], []
    for g in range(N_ATT_GROUPS):
        h_g = matmul(xb_views[g].reshape(t, D_MODEL), lw['w_att'][g], BF16)
        o_g, l_g = dilated_attention_group(h_g, g)
        outs.append(o_g)
        lses.append(l_g)
    o_b = attention_merge(outs, lses)

    o_c = conv_module(h_conv, lw['conv_w'], lw['conv_b'], lw['conv_ln_g'], lw['conv_ln_b'])

    merged = gated_merge(xb, o_a, o_b, o_c, lw['w_gate3'], lw['b_gate3'], lw['w_branch'])
    return matmul(merged, lw['w_out'], BF16)


def expert_choice_moe(x, logits, lw):
    n_tok = x.shape[0]
    cap = max(1, EC_FACTOR * n_tok // N_EXPERTS)
    probs = jax.nn.softmax(logits[:, :N_EXPERTS], axis=-1)
    gate, idx = lax.top_k(probs.T, cap)
    return moe_scatter_add(x, idx, gate, lw['w_e1'], lw['w_e3'], lw['w_e2'])


def encoder_trunk(x, layers):
    x = x[0]
    xb_views = cast_views(x)
    for li, lw in enumerate(layers):
        y = mixer_block(xb_views, lw)
        x, logits = add_ln_router(x, y, lw['ln1_g'], lw['ln1_b'], lw['w_router'])
        m = expert_choice_moe(x, logits, lw)
        x, *xb_views = add_ln_views(x, m, lw['ln2_g'], lw['ln2_b'], with_views=li + 1 < len(layers))
    return x[None]


def _prep_layer(l, w_in, w_gla_gate, b_gla_gate, gla_norm_g, conv_w, conv_b, conv_ln_g, conv_ln_b,
                w_branch, w_gate, b_gate, w_out, ln1_g, ln1_b, w_router, w_e1, w_e3, w_e2, ln2_g, ln2_b):
    o_ra, o_att, o_conv = IN_OFFSETS[3], IN_OFFSETS[4], IN_OFFSETS[7]
    wi = w_in[l]
    wz = jnp.zeros((2, LANES, GLA_QK), F32)
    wz = wz.at[0, :GLA_RANK].set(w_gla_gate[l, 0]).at[1, GLA_RANK:2 * GLA_RANK].set(w_gla_gate[l, 1])
    w_att = wi[:, o_att:o_conv].reshape(D_MODEL, 3, N_ATT_GROUPS, ATT_OUT).transpose(2, 0, 1, 3)
    row = lambda a: a.reshape(1, -1)
    return {
        'w_gla': wi[:, :o_ra].astype(BF16),
        'w_ra': jnp.pad(wi[:, o_ra:o_att], ((0, 0), (0, LANES - 2 * GLA_RANK))).astype(BF16),
        'w_att': w_att.reshape(N_ATT_GROUPS, D_MODEL, 3 * ATT_OUT).astype(BF16),
        'w_conv': wi[:, o_conv:].astype(BF16),
        'wz': wz.astype(BF16), 'bz': b_gla_gate[l].reshape(2, 1, GLA_QK),
        'gla_norm_g': row(gla_norm_g[l]),
        'conv_w': conv_w[l], 'conv_b': row(conv_b[l]), 'conv_ln_g': row(conv_ln_g[l]), 'conv_ln_b': row(conv_ln_b[l]),
        'w_branch': w_branch[l].astype(BF16),
        'w_gate3': w_gate[l].reshape(D_MODEL, N_BRANCH, D_MODEL).transpose(1, 0, 2).astype(BF16),
        'b_gate3': b_gate[l].reshape(N_BRANCH, 1, D_MODEL),
        'w_out': w_out[l].astype(BF16),
        'ln1_g': row(ln1_g[l]), 'ln1_b': row(ln1_b[l]),
        'w_router': jnp.pad(w_router[l], ((0, 0), (0, LANES - N_EXPERTS))).astype(BF16),
        'w_e1': w_e1[l].astype(BF16), 'w_e3': w_e3[l].astype(BF16), 'w_e2': w_e2[l].astype(BF16),
        'ln2_g': row(ln2_g[l]), 'ln2_b': row(ln2_b[l]),
    }


def kernel(x_prompt, x_sample, w_in, w_gla_gate, b_gla_gate, gla_norm_g, conv_w, conv_b, conv_ln_g, conv_ln_b, w_branch, w_gate, b_gate, w_out, ln1_g, ln1_b, w_router, w_e1, w_e3, w_e2, ln2_g, ln2_b):
    params = (w_in, w_gla_gate, b_gla_gate, gla_norm_g, conv_w, conv_b, conv_ln_g, conv_ln_b,
              w_branch, w_gate, b_gate, w_out, ln1_g, ln1_b, w_router, w_e1, w_e3, w_e2, ln2_g, ln2_b)
    layers = [_prep_layer(l, *params) for l in range(DEPTH)]
    return (encoder_trunk(x_prompt, layers), encoder_trunk(x_sample, layers))
```
